```python
import math
import jax, jax.numpy as jnp
from jax import lax
import numpy as np

D_MODEL = 1024
BATCH = 16
SEQ = 2048
DEPTH = 2

PLE_DIM = 256
ATTN_HEADS = 16
HEAD_DIM = 64
ATTN_WIDTH = ATTN_HEADS * HEAD_DIM
POOL_WINDOWS = (2, 4, 8, 16)
N_POOL_GROUPS = len(POOL_WINDOWS)
POOL_GROUP_DIM = 256
POOL_WIDTH = N_POOL_GROUPS * POOL_GROUP_DIM
MIX_WIDTH = ATTN_WIDTH + POOL_WIDTH
Q_BLOCK = 128
EPS = 1e-6
SPLIT_SIZES = (ATTN_WIDTH, ATTN_WIDTH, ATTN_WIDTH, ATTN_WIDTH, ATTN_HEADS, POOL_WIDTH, POOL_WIDTH)
IN_COLS = sum(SPLIT_SIZES)
SPLIT_POINTS = tuple(int(v) for v in np.cumsum(SPLIT_SIZES)[:-1])

kernel_name = "hymba_fox_pool_hybrid"


def rms_norm(x, g):
    xf = x.astype(jnp.float32)
    y = xf * lax.rsqrt(jnp.mean(xf * xf, axis=-1, keepdims=True) + EPS)
    return (y * g.astype(jnp.float32)).astype(x.dtype)


def multi_scale_pool(u, w_pool, pool_scale):
    B, S, _ = u.shape
    ug = u.reshape(B, S, N_POOL_GROUPS, POOL_GROUP_DIM).astype(jnp.float32)
    cs = jnp.cumsum(ug, axis=1)
    pos = jnp.arange(1, S + 1, dtype=jnp.float32)
    outs = []
    for g, w in enumerate(POOL_WINDOWS):
        c = cs[:, :, g]
        shifted = jnp.pad(c, ((0, 0), (w, 0), (0, 0)))[:, :S]
        count = jnp.minimum(pos, float(w))[None, :, None]
        outs.append((c - shifted) / count - ug[:, :, g])
    pooled = jnp.stack(outs, axis=2).astype(u.dtype)
    mixed = jnp.einsum('bsgc,gcd->bsgd', pooled, w_pool)
    return mixed.reshape(B, S, POOL_WIDTH) * pool_scale


def forgetting_attention(q, k, v, log_f):
    B, S, H, Dh = q.shape
    scale = 1.0 / math.sqrt(Dh)
    c = jnp.cumsum(log_f, axis=1).transpose(0, 2, 1)
    qh = q.transpose(0, 2, 1, 3)
    kh = k.transpose(0, 2, 1, 3)
    vh = v.transpose(0, 2, 1, 3)
    tri = jnp.tril(jnp.ones((Q_BLOCK, Q_BLOCK), dtype=bool))
    outs = []
    for i in range(S // Q_BLOCK):
        q0 = i * Q_BLOCK
        end = q0 + Q_BLOCK
        qb = qh[:, :, q0:end]
        kb = kh[:, :, :end]
        vb = vh[:, :, :end]
        s = jnp.einsum('bhqd,bhkd->bhqk', qb, kb,
                       preferred_element_type=jnp.float32) * scale
        s = s + c[:, :, q0:end, None] - c[:, :, None, :end]
        mask = jnp.concatenate([jnp.ones((Q_BLOCK, q0), dtype=bool), tri], axis=1)
        s = jnp.where(mask[None, None], s, -jnp.inf)
        pr = jax.nn.softmax(s, axis=-1)
        outs.append(jnp.einsum('bhqk,bhkd->bhqd', pr.astype(v.dtype), vb))
    o = jnp.concatenate(outs, axis=2)
    return o.transpose(0, 2, 1, 3)


def setup_inputs(seed: int = 0) -> dict:
    key = jax.random.key(seed)
    ks = jax.random.split(key, 12)
    f32 = jnp.float32
    x = jax.random.normal(ks[0], (BATCH, SEQ, D_MODEL), f32)
    p = jax.random.normal(ks[1], (DEPTH, BATCH, SEQ, PLE_DIM), f32)
    norm_pre = 1.0 + 0.05 * jax.random.normal(ks[2], (DEPTH, D_MODEL), f32)
    norm_post = 1.0 + 0.05 * jax.random.normal(ks[3], (DEPTH, D_MODEL), f32)
    w_in = jax.random.normal(ks[4], (DEPTH, D_MODEL, IN_COLS), f32) * D_MODEL ** -0.5
    b_f = 3.0 + 0.5 * jax.random.normal(ks[5], (DEPTH, ATTN_HEADS), f32)
    w_pool = jax.random.normal(ks[6], (DEPTH, N_POOL_GROUPS, POOL_GROUP_DIM, POOL_GROUP_DIM), f32) * POOL_GROUP_DIM ** -0.5
    pool_scale = 1.0 + 0.1 * jax.random.normal(ks[7], (DEPTH, POOL_WIDTH), f32)
    w_out = jax.random.normal(ks[8], (DEPTH, MIX_WIDTH, D_MODEL), f32) * MIX_WIDTH ** -0.5
    w_pg = jax.random.normal(ks[9], (DEPTH, D_MODEL, D_MODEL), f32) * D_MODEL ** -0.5
    w_pe = jax.random.normal(ks[10], (DEPTH, PLE_DIM, D_MODEL), f32) * (0.5 * PLE_DIM ** -0.5)
    return {"x": x, "p": p, "norm_pre": norm_pre, "norm_post": norm_post,
            "w_in": w_in, "b_f": b_f, "w_pool": w_pool, "pool_scale": pool_scale,
            "w_out": w_out, "w_pg": w_pg, "w_pe": w_pe}


def reference(x, p, norm_pre, norm_post, w_in, b_f, w_pool, pool_scale, w_out, w_pg, w_pe):
    B, S, _ = x.shape
    h = x
    for i in range(DEPTH):
        hn = rms_norm(h, norm_pre[i])
        proj = hn @ w_in[i]
        q, k, v, z_attn, f_logit, u_pool, z_pool = jnp.split(proj, SPLIT_POINTS, axis=-1)
        log_f = jax.nn.log_sigmoid(f_logit.astype(jnp.float32) + b_f[i].astype(jnp.float32))
        attn = forgetting_attention(
            q.reshape(B, S, ATTN_HEADS, HEAD_DIM),
            k.reshape(B, S, ATTN_HEADS, HEAD_DIM),
            v.reshape(B, S, ATTN_HEADS, HEAD_DIM), log_f)
        attn = attn.reshape(B, S, ATTN_WIDTH) * jax.nn.silu(z_attn)
        pool = multi_scale_pool(u_pool, w_pool[i], pool_scale[i]) * jax.nn.silu(z_pool)
        mix = jnp.concatenate([attn, pool], axis=-1) @ w_out[i]
        h = h + rms_norm(mix, norm_post[i])
        gate = jax.nn.sigmoid(h @ w_pg[i])
        h = h + gate * (p[i] @ w_pe[i])
    return h
```

```python
import functools

import jax
import jax.numpy as jnp
from jax import lax
from jax.experimental import pallas as pl
from jax.experimental.pallas import tpu as pltpu

F32 = jnp.float32
BF16 = jnp.bfloat16

D_MODEL = 1024
PLE_DIM = 256
HEADS = 16
HEAD_DIM = 64
ATTN_WIDTH = HEADS * HEAD_DIM
POOL_WINDOWS = (2, 4, 8, 16)
POOL_GROUP_DIM = 256
POOL_WIDTH = len(POOL_WINDOWS) * POOL_GROUP_DIM
EPS = 1e-6
QK_SCALE = 0.125

LANES = 128
HALO = 16
MAIN_COLS = 4 * ATTN_WIDTH + 2 * POOL_WIDTH
COL_Q, COL_K, COL_V, COL_ZA, COL_U, COL_ZP = range(6)

VMEM_LIMIT = 56 * 1024 * 1024

PROJ_TM = 1024
PROJ_TN = 512
ATT_TQ = 256
MIX_TM = 512


def _rms_norm(x, g):
    return x * lax.rsqrt(jnp.mean(x * x, axis=-1, keepdims=True) + EPS) * g


def _silu(z):
    return z * jax.nn.sigmoid(z)


def _in_proj_kernel(h_ref, g_ref, w_ref, wf_ref, proj_ref, flog_ref, hn_ref):
    @pl.when(pl.program_id(1) == 0)
    def _():
        hn = _rms_norm(h_ref[...], g_ref[...]).astype(BF16)
        hn_ref[...] = hn
        flog_ref[...] = jnp.dot(hn, wf_ref[...], preferred_element_type=F32)

    proj_ref[...] = jnp.dot(hn_ref[...], w_ref[...], preferred_element_type=F32).astype(BF16)


def _in_proj(h2d, g, w_main, w_f):
    t = h2d.shape[0]
    grid = (t // PROJ_TM, MAIN_COLS // PROJ_TN)
    return pl.pallas_call(
        _in_proj_kernel,
        grid=grid,
        in_specs=[
            pl.BlockSpec((PROJ_TM, D_MODEL), lambda i, j: (i, 0)),
            pl.BlockSpec((1, D_MODEL), lambda i, j: (0, 0)),
            pl.BlockSpec((D_MODEL, PROJ_TN), lambda i, j: (0, j)),
            pl.BlockSpec((D_MODEL, LANES), lambda i, j: (0, 0)),
        ],
        out_specs=[
            pl.BlockSpec((PROJ_TM, PROJ_TN), lambda i, j: (i, j)),
            pl.BlockSpec((PROJ_TM, LANES), lambda i, j: (i, 0)),
        ],
        out_shape=[
            jax.ShapeDtypeStruct((t, MAIN_COLS), BF16),
            jax.ShapeDtypeStruct((t, LANES), F32),
        ],
        scratch_shapes=[pltpu.VMEM((PROJ_TM, D_MODEL), BF16)],
        compiler_params=pltpu.CompilerParams(
            dimension_semantics=("arbitrary", "arbitrary"), vmem_limit_bytes=VMEM_LIMIT),
        name="in_proj",
    )(h2d, g, w_main, w_f)


def _gate_cumsum_kernel(flog_ref, bf_ref, c_ref):
    x = flog_ref[0] + bf_ref[...]
    c = jnp.minimum(x, 0.0) - jnp.log1p(jnp.exp(-jnp.abs(x)))
    seq = c.shape[0]
    row = lax.broadcasted_iota(jnp.int32, c.shape, 0)
    span = 1
    while span < seq:
        c = c + jnp.where(row >= span, pltpu.roll(c, span, 0), 0.0)
        span *= 2
    c_ref[0] = c


def _gate_cumsum(flog, bf):
    b, s, _ = flog.shape
    return pl.pallas_call(
        _gate_cumsum_kernel,
        grid=(b,),
        in_specs=[
            pl.BlockSpec((1, s, LANES), lambda i: (i, 0, 0)),
            pl.BlockSpec((1, LANES), lambda i: (0, 0)),
        ],
        out_specs=pl.BlockSpec((1, s, LANES), lambda i: (i, 0, 0)),
        out_shape=jax.ShapeDtypeStruct((b, s, LANES), F32),
        compiler_params=pltpu.CompilerParams(
            dimension_semantics=("arbitrary",), vmem_limit_bytes=VMEM_LIMIT),
        name="gate_cumsum",
    )(flog, bf)


_NT = (((1,), (1,)), ((), ()))


def _fold(x, op):
    out = x[:, :LANES]
    for g in range(1, x.shape[1] // LANES):
        out = op(out, x[:, g * LANES:(g + 1) * LANES])
    return out


def _attn_kernel(q_ref, k_ref, v_ref, z_ref, c_ref, o_ref, qa_ref, ka_ref, s_ref, acc_ref, l_ref):
    seq = q_ref.shape[1]
    tq = ATT_TQ
    pair = pl.program_id(1)
    lane = lax.broadcasted_iota(jnp.int32, (seq, LANES), 1)
    c_all = c_ref[0]
    q = q_ref[0].astype(F32) * QK_SCALE
    k = k_ref[0].astype(F32)

    for h in range(2):
        ch = jnp.sum(jnp.where(lane == 2 * pair + h, c_all, 0.0), axis=1, keepdims=True)
        hi = ch.astype(BF16).astype(F32)
        mid = (ch - hi).astype(BF16).astype(F32)
        lo = (ch - hi - mid).astype(BF16).astype(F32)
        data = (lane < HEAD_DIM) if h == 0 else (lane >= HEAD_DIM)
        a = lane - (HEAD_DIM if h == 0 else 0)
        qa = jnp.where(a == 0, hi, jnp.where(a == 1, mid, jnp.where(a == 2, lo,
             jnp.where((a >= 3) & (a <= 5), 1.0, 0.0))))
        ka = jnp.where(a == 3, -hi, jnp.where(a == 4, -mid, jnp.where(a == 5, -lo,
             jnp.where((a >= 0) & (a <= 2), 1.0, 0.0))))
        qa_ref[h] = jnp.where(data, q, qa).astype(BF16)
        ka_ref[h] = jnp.where(data, k, ka).astype(BF16)

    tri = (lax.broadcasted_iota(jnp.int32, (tq, tq), 1)
           <= lax.broadcasted_iota(jnp.int32, (tq, tq), 0))
    lane_t = lax.broadcasted_iota(jnp.int32, (tq, LANES), 1)

    def q_tile(qi, carry):
        r0 = pl.multiple_of(qi * tq, tq)
        outs = []
        for h in range(2):
            qt = qa_ref[h, pl.ds(r0, tq), :]

            def scores(kc, m_run):
                k0 = pl.multiple_of(kc * tq, tq)
                s = lax.dot_general(qt, ka_ref[h, pl.ds(k0, tq), :], _NT,
                                    preferred_element_type=F32)
                s_ref[kc] = s
                return jnp.maximum(m_run, _fold(s, jnp.maximum))

            m_run = lax.fori_loop(0, qi, scores, jnp.full((tq, LANES), -jnp.inf, F32))
            s = lax.dot_general(qt, ka_ref[h, pl.ds(r0, tq), :], _NT, preferred_element_type=F32)
            s = jnp.where(tri, s, -jnp.inf)
            s_ref[qi] = s
            m_run = jnp.maximum(m_run, _fold(s, jnp.maximum))
            m = jnp.max(m_run, axis=1, keepdims=True)

            acc_ref[...] = jnp.zeros_like(acc_ref)
            l_ref[...] = jnp.zeros_like(l_ref)

            def weighted(kc, carry2):
                k0 = pl.multiple_of(kc * tq, tq)
                p = jnp.exp(s_ref[kc] - m)
                l_ref[...] += _fold(p, jnp.add)
                acc_ref[...] += jnp.dot(p.astype(BF16), v_ref[0, pl.ds(k0, tq), :],
                                        preferred_element_type=F32)
                return carry2

            lax.fori_loop(0, qi + 1, weighted, 0)
            outs.append(acc_ref[...] / jnp.sum(l_ref[...], axis=1, keepdims=True))
        o = jnp.where(lane_t < HEAD_DIM, outs[0], outs[1])
        z = z_ref[0, pl.ds(r0, tq), :].astype(F32)
        o_ref[0, pl.ds(r0, tq), :] = (o * _silu(z)).astype(BF16)
        return carry

    lax.fori_loop(0, seq // tq, q_tile, 0)


def _attention(proj, c):
    b, s, _ = proj.shape
    pairs = ATTN_WIDTH // LANES
    blocks_per_col = ATTN_WIDTH // LANES

    def col_spec(col):
        return pl.BlockSpec((1, s, LANES), lambda i, j: (i, 0, col * blocks_per_col + j))

    return pl.pallas_call(
        _attn_kernel,
        grid=(b, pairs),
        in_specs=[col_spec(COL_Q), col_spec(COL_K), col_spec(COL_V), col_spec(COL_ZA),
                  pl.BlockSpec((1, s, LANES), lambda i, j: (i, 0, 0))],
        out_specs=pl.BlockSpec((1, s, LANES), lambda i, j: (i, 0, j)),
        out_shape=jax.ShapeDtypeStruct((b, s, ATTN_WIDTH), BF16),
        scratch_shapes=[
            pltpu.VMEM((2, s, LANES), BF16),
            pltpu.VMEM((2, s, LANES), BF16),
            pltpu.VMEM((s // ATT_TQ, ATT_TQ, ATT_TQ), F32),
            pltpu.VMEM((ATT_TQ, LANES), F32),
            pltpu.VMEM((ATT_TQ, LANES), F32),
        ],
        compiler_params=pltpu.CompilerParams(
            dimension_semantics=("arbitrary", "arbitrary"), vmem_limit_bytes=VMEM_LIMIT),
        name="fox_attn",
    )(proj, proj, proj, proj, c)


def _mix_kernel(g_ref, u_ref, halo_ref, zp_ref, h_ref, p_ref, wpool_ref, ps_ref, wout_ref,
                gpost_ref, wpg_ref, wpe_ref, out_ref):
    tm = u_ref.shape[1]
    i = pl.program_id(1)
    u = u_ref[0].astype(F32)
    halo = jnp.where(i > 0, halo_ref[0].astype(F32), 0.0)
    ext = jnp.concatenate([halo, u], axis=0)
    pos = (i * tm + lax.broadcasted_iota(jnp.int32, (tm, 1), 0) + 1).astype(F32)

    mixed = []
    for g, w in enumerate(POOL_WINDOWS):
        cols = slice(g * POOL_GROUP_DIM, (g + 1) * POOL_GROUP_DIM)
        win = ext[:, cols]
        span = 1
        while span < w:
            win = win + pltpu.roll(win, span, 0)
            span *= 2
        inv_count = 1.0 / jnp.minimum(pos, float(w))
        pooled = win[HALO:] * inv_count - u[:, cols]
        mixed.append(jnp.dot(pooled.astype(BF16), wpool_ref[g], preferred_element_type=F32))
    zp = zp_ref[0].astype(F32)
    pool_out = (jnp.concatenate(mixed, axis=1) * ps_ref[...] * _silu(zp)).astype(BF16)

    mix = jnp.dot(g_ref[0], wout_ref[:ATTN_WIDTH, :], preferred_element_type=F32)
    mix = mix + jnp.dot(pool_out, wout_ref[ATTN_WIDTH:, :], preferred_element_type=F32)
    h1 = h_ref[0] + _rms_norm(mix, gpost_ref[...])
    gate = jax.nn.sigmoid(jnp.dot(h1.astype(BF16), wpg_ref[...], preferred_element_type=F32))
    pe = jnp.dot(p_ref[0].astype(BF16), wpe_ref[...], preferred_element_type=F32)
    out_ref[0] = h1 + gate * pe


def _mix_out(g, proj, h, p, w_pool, pool_scale, w_out, g_post, w_pg, w_pe):
    b, s, _ = h.shape
    tm = MIX_TM
    halo_per_tile = tm // HALO

    def resident(shape):
        return pl.BlockSpec(shape, lambda i, j: (0,) * len(shape), pipeline_mode=pl.Buffered(1))

    return pl.pallas_call(
        _mix_kernel,
        grid=(b, s // tm),
        in_specs=[
            pl.BlockSpec((1, tm, ATTN_WIDTH), lambda i, j: (i, j, 0)),
            pl.BlockSpec((1, tm, POOL_WIDTH), lambda i, j: (i, j, COL_U)),
            pl.BlockSpec((1, HALO, POOL_WIDTH),
                         lambda i, j: (i, jnp.maximum(j * halo_per_tile - 1, 0), COL_U)),
            pl.BlockSpec((1, tm, POOL_WIDTH), lambda i, j: (i, j, COL_ZP)),
            pl.BlockSpec((1, tm, D_MODEL), lambda i, j: (i, j, 0)),
            pl.BlockSpec((1, tm, PLE_DIM), lambda i, j: (i, j, 0)),
            resident(w_pool.shape),
            resident(pool_scale.shape),
            resident(w_out.shape),
            resident(g_post.shape),
            resident(w_pg.shape),
            resident(w_pe.shape),
        ],
        out_specs=pl.BlockSpec((1, tm, D_MODEL), lambda i, j: (i, j, 0)),
        out_shape=jax.ShapeDtypeStruct((b, s, D_MODEL), F32),
        compiler_params=pltpu.CompilerParams(
            dimension_semantics=("arbitrary", "arbitrary"), vmem_limit_bytes=VMEM_LIMIT),
        name="mix_out",
    )(g, proj, proj, proj, h, p, w_pool, pool_scale, w_out, g_post, w_pg, w_pe)


def kernel(x, p, norm_pre, norm_post, w_in, b_f, w_pool, pool_scale, w_out, w_pg, w_pe):
    b, s, d = x.shape
    depth = w_in.shape[0]
    f_lo = 4 * ATTN_WIDTH
    f_hi = f_lo + HEADS
    h = x
    for i in range(depth):
        w = w_in[i]
        w_main = jnp.concatenate([w[:, :f_lo], w[:, f_hi:]], axis=1).astype(BF16)
        w_f = jnp.pad(w[:, f_lo:f_hi], ((0, 0), (0, LANES - HEADS))).astype(BF16)
        bf = jnp.pad(b_f[i], (0, LANES - HEADS)).reshape(1, LANES)
        proj, flog = _in_proj(h.reshape(b * s, d), norm_pre[i].reshape(1, d), w_main, w_f)
        c = _gate_cumsum(flog.reshape(b, s, LANES), bf)
        proj = proj.reshape(b, s, MAIN_COLS)
        g = _attention(proj, c)
        h = _mix_out(g, proj, h, p[i], w_pool[i].astype(BF16), pool_scale[i].reshape(1, POOL_WIDTH),
                     w_out[i].astype(BF16), norm_post[i].reshape(1, d), w_pg[i].astype(BF16),
                     w_pe[i].astype(BF16))
    return h
```

```python
import functools

import jax
import jax.numpy as jnp
from jax import lax
from jax.experimental import pallas as pl
from jax.experimental.pallas import tpu as pltpu

F32 = jnp.float32
BF16 = jnp.bfloat16

D_MODEL = 1024
PLE_DIM = 256
HEADS = 16
HEAD_DIM = 64
ATTN_WIDTH = HEADS * HEAD_DIM
POOL_WINDOWS = (2, 4, 8, 16)
POOL_GROUP_DIM = 256
POOL_WIDTH = len(POOL_WINDOWS) * POOL_GROUP_DIM
EPS = 1e-6
QK_SCALE = 0.125

LANES = 128
HALO = 16
MAIN_COLS = 4 * ATTN_WIDTH + 2 * POOL_WIDTH
COL_Q, COL_K, COL_V, COL_ZA, COL_U, COL_ZP = range(6)

VMEM_LIMIT = 56 * 1024 * 1024

PROJ_TM = 1024
PROJ_TN = 512
ATT_TQ = 256
MIX_TM = 512


def _rms_norm(x, g):
    return x * lax.rsqrt(jnp.mean(x * x, axis=-1, keepdims=True) + EPS) * g


def _silu(z):
    return z * jax.nn.sigmoid(z)


def _in_proj_kernel(h_ref, g_ref, w_ref, wf_ref, proj_ref, flog_ref, hn_ref):
    @pl.when(pl.program_id(1) == 0)
    def _():
        hn = _rms_norm(h_ref[...], g_ref[...]).astype(BF16)
        hn_ref[...] = hn
        flog_ref[...] = jnp.dot(hn, wf_ref[...], preferred_element_type=F32)

    proj_ref[...] = jnp.dot(hn_ref[...], w_ref[...], preferred_element_type=F32).astype(BF16)


def _in_proj(h2d, g, w_main, w_f):
    t = h2d.shape[0]
    grid = (t // PROJ_TM, MAIN_COLS // PROJ_TN)
    return pl.pallas_call(
        _in_proj_kernel,
        grid=grid,
        in_specs=[
            pl.BlockSpec((PROJ_TM, D_MODEL), lambda i, j: (i, 0)),
            pl.BlockSpec((1, D_MODEL), lambda i, j: (0, 0)),
            pl.BlockSpec((D_MODEL, PROJ_TN), lambda i, j: (0, j)),
            pl.BlockSpec((D_MODEL, LANES), lambda i, j: (0, 0)),
        ],
        out_specs=[
            pl.BlockSpec((PROJ_TM, PROJ_TN), lambda i, j: (i, j)),
            pl.BlockSpec((PROJ_TM, LANES), lambda i, j: (i, 0)),
        ],
        out_shape=[
            jax.ShapeDtypeStruct((t, MAIN_COLS), BF16),
            jax.ShapeDtypeStruct((t, LANES), F32),
        ],
        scratch_shapes=[pltpu.VMEM((PROJ_TM, D_MODEL), BF16)],
        compiler_params=pltpu.CompilerParams(
            dimension_semantics=("arbitrary", "arbitrary"), vmem_limit_bytes=VMEM_LIMIT),
        name="in_proj",
    )(h2d, g, w_main, w_f)


def _gate_cumsum_kernel(flog_ref, bf_ref, c_ref):
    x = flog_ref[0] + bf_ref[...]
    c = jnp.minimum(x, 0.0) - jnp.log1p(jnp.exp(-jnp.abs(x)))
    seq = c.shape[0]
    row = lax.broadcasted_iota(jnp.int32, c.shape, 0)
    span = 1
    while span < seq:
        c = c + jnp.where(row >= span, pltpu.roll(c, span, 0), 0.0)
        span *= 2
    c_ref[0] = c


def _gate_cumsum(flog, bf):
    b, s, _ = flog.shape
    return pl.pallas_call(
        _gate_cumsum_kernel,
        grid=(b,),
        in_specs=[
            pl.BlockSpec((1, s, LANES), lambda i: (i, 0, 0)),
            pl.BlockSpec((1, LANES), lambda i: (0, 0)),
        ],
        out_specs=pl.BlockSpec((1, s, LANES), lambda i: (i, 0, 0)),
        out_shape=jax.ShapeDtypeStruct((b, s, LANES), F32),
        compiler_params=pltpu.CompilerParams(
            dimension_semantics=("arbitrary",), vmem_limit_bytes=VMEM_LIMIT),
        name="gate_cumsum",
    )(flog, bf)


_NT = (((1,), (1,)), ((), ()))


def _fold(x, op):
    out = x[:, :LANES]
    for g in range(1, x.shape[1] // LANES):
        out = op(out, x[:, g * LANES:(g + 1) * LANES])
    return out


def _attn_kernel(q_ref, k_ref, v_ref, z_ref, c_ref, o_ref, qa_ref, ka_ref, s_ref, p_ref):
    seq = q_ref.shape[1]
    tq = ATT_TQ
    pair = pl.program_id(1)
    lane = lax.broadcasted_iota(jnp.int32, (seq, LANES), 1)
    c_all = c_ref[0]
    q = q_ref[0].astype(F32) * QK_SCALE
    k = k_ref[0].astype(F32)

    for h in range(2):
        ch = jnp.sum(jnp.where(lane == 2 * pair + h, c_all, 0.0), axis=1, keepdims=True)
        hi = ch.astype(BF16).astype(F32)
        mid = (ch - hi).astype(BF16).astype(F32)
        lo = (ch - hi - mid).astype(BF16).astype(F32)
        data = (lane < HEAD_DIM) if h == 0 else (lane >= HEAD_DIM)
        a = lane - (HEAD_DIM if h == 0 else 0)
        qa = jnp.where(a == 0, hi, jnp.where(a == 1, mid, jnp.where(a == 2, lo,
             jnp.where((a >= 3) & (a <= 5), 1.0, 0.0))))
        ka = jnp.where(a == 3, -hi, jnp.where(a == 4, -mid, jnp.where(a == 5, -lo,
             jnp.where((a >= 0) & (a <= 2), 1.0, 0.0))))
        qa_ref[h] = jnp.where(data, q, qa).astype(BF16)
        ka_ref[h] = jnp.where(data, k, ka).astype(BF16)

    tri = (lax.broadcasted_iota(jnp.int32, (tq, tq), 1)
           <= lax.broadcasted_iota(jnp.int32, (tq, tq), 0))
    lane_t = lax.broadcasted_iota(jnp.int32, (tq, LANES), 1)

    for qi in range(seq // tq):
        rows = slice(qi * tq, (qi + 1) * tq)
        prefix = (qi + 1) * tq
        outs = []
        for h in range(2):
            buf = 2 * (qi % 2) + h
            qt = qa_ref[h, rows, :]
            m_run = None
            for kc in range(qi + 1):
                cols = slice(kc * tq, (kc + 1) * tq)
                s = lax.dot_general(qt, ka_ref[h, cols, :], _NT, preferred_element_type=F32)
                if kc == qi:
                    s = jnp.where(tri, s, -jnp.inf)
                s_ref[buf, :, cols] = s
                fm = _fold(s, jnp.maximum)
                m_run = fm if m_run is None else jnp.maximum(m_run, fm)
            m = jnp.max(m_run, axis=1, keepdims=True)
            l_run = None
            for kc in range(qi + 1):
                cols = slice(kc * tq, (kc + 1) * tq)
                p = jnp.exp(s_ref[buf, :, cols] - m)
                p_ref[buf, :, cols] = p.astype(BF16)
                fl = _fold(p, jnp.add)
                l_run = fl if l_run is None else l_run + fl
            acc = jnp.dot(p_ref[buf, :, :prefix], v_ref[0, :prefix, :],
                          preferred_element_type=F32)
            outs.append(acc / jnp.sum(l_run, axis=1, keepdims=True))
        o = jnp.where(lane_t < HEAD_DIM, outs[0], outs[1])
        z = z_ref[0, rows, :].astype(F32)
        o_ref[0, rows, :] = (o * _silu(z)).astype(BF16)


def _attention(proj, c):
    b, s, _ = proj.shape
    pairs = ATTN_WIDTH // LANES
    blocks_per_col = ATTN_WIDTH // LANES

    def col_spec(col):
        return pl.BlockSpec((1, s, LANES), lambda i, j: (i, 0, col * blocks_per_col + j))

    return pl.pallas_call(
        _attn_kernel,
        grid=(b, pairs),
        in_specs=[col_spec(COL_Q), col_spec(COL_K), col_spec(COL_V), col_spec(COL_ZA),
                  pl.BlockSpec((1, s, LANES), lambda i, j: (i, 0, 0))],
        out_specs=pl.BlockSpec((1, s, LANES), lambda i, j: (i, 0, j)),
        out_shape=jax.ShapeDtypeStruct((b, s, ATTN_WIDTH), BF16),
        scratch_shapes=[
            pltpu.VMEM((2, s, LANES), BF16),
            pltpu.VMEM((2, s, LANES), BF16),
            pltpu.VMEM((4, ATT_TQ, s), F32),
            pltpu.VMEM((4, ATT_TQ, s), BF16),
        ],
        compiler_params=pltpu.CompilerParams(
            dimension_semantics=("arbitrary", "arbitrary"), vmem_limit_bytes=VMEM_LIMIT),
        name="fox_attn",
    )(proj, proj, proj, proj, c)


def _mix_kernel(g_ref, u_ref, halo_ref, zp_ref, h_ref, p_ref, wpool_ref, ps_ref, wout_ref,
                gpost_ref, wpg_ref, wpe_ref, out_ref):
    tm = u_ref.shape[1]
    i = pl.program_id(1)
    u = u_ref[0].astype(F32)
    halo = jnp.where(i > 0, halo_ref[0].astype(F32), 0.0)
    ext = jnp.concatenate([halo, u], axis=0)
    pos = (i * tm + lax.broadcasted_iota(jnp.int32, (tm, 1), 0) + 1).astype(F32)

    mixed = []
    for g, w in enumerate(POOL_WINDOWS):
        cols = slice(g * POOL_GROUP_DIM, (g + 1) * POOL_GROUP_DIM)
        win = ext[:, cols]
        span = 1
        while span < w:
            win = win + pltpu.roll(win, span, 0)
            span *= 2
        inv_count = 1.0 / jnp.minimum(pos, float(w))
        pooled = win[HALO:] * inv_count - u[:, cols]
        mixed.append(jnp.dot(pooled.astype(BF16), wpool_ref[g], preferred_element_type=F32))
    zp = zp_ref[0].astype(F32)
    pool_out = (jnp.concatenate(mixed, axis=1) * ps_ref[...] * _silu(zp)).astype(BF16)

    mix = jnp.dot(g_ref[0], wout_ref[:ATTN_WIDTH, :], preferred_element_type=F32)
    mix = mix + jnp.dot(pool_out, wout_ref[ATTN_WIDTH:, :], preferred_element_type=F32)
    h1 = h_ref[0] + _rms_norm(mix, gpost_ref[...])
    gate = jax.nn.sigmoid(jnp.dot(h1.astype(BF16), wpg_ref[...], preferred_element_type=F32))
    pe = jnp.dot(p_ref[0].astype(BF16), wpe_ref[...], preferred_element_type=F32)
    out_ref[0] = h1 + gate * pe


def _mix_out(g, proj, h, p, w_pool, pool_scale, w_out, g_post, w_pg, w_pe):
    b, s, _ = h.shape
    tm = MIX_TM
    halo_per_tile = tm // HALO

    def resident(shape):
        return pl.BlockSpec(shape, lambda i, j: (0,) * len(shape), pipeline_mode=pl.Buffered(1))

    return pl.pallas_call(
        _mix_kernel,
        grid=(b, s // tm),
        in_specs=[
            pl.BlockSpec((1, tm, ATTN_WIDTH), lambda i, j: (i, j, 0)),
            pl.BlockSpec((1, tm, POOL_WIDTH), lambda i, j: (i, j, COL_U)),
            pl.BlockSpec((1, HALO, POOL_WIDTH),
                         lambda i, j: (i, jnp.maximum(j * halo_per_tile - 1, 0), COL_U)),
            pl.BlockSpec((1, tm, POOL_WIDTH), lambda i, j: (i, j, COL_ZP)),
            pl.BlockSpec((1, tm, D_MODEL), lambda i, j: (i, j, 0)),
            pl.BlockSpec((1, tm, PLE_DIM), lambda i, j: (i, j, 0)),
            resident(w_pool.shape),
            resident(pool_scale.shape),
            resident(w_out.shape),
            resident(g_post.shape),
            resident(w_pg.shape),
            resident(w_pe.shape),
        ],
        out_specs=pl.BlockSpec((1, tm, D_MODEL), lambda i, j: (i, j, 0)),
        out_shape=jax.ShapeDtypeStruct((b, s, D_MODEL), F32),
        compiler_params=pltpu.CompilerParams(
            dimension_semantics=("arbitrary", "arbitrary"), vmem_limit_bytes=VMEM_LIMIT),
        name="mix_out",
    )(g, proj, proj, proj, h, p, w_pool, pool_scale, w_out, g_post, w_pg, w_pe)


def kernel(x, p, norm_pre, norm_post, w_in, b_f, w_pool, pool_scale, w_out, w_pg, w_pe):
    b, s, d = x.shape
    depth = w_in.shape[0]
    f_lo = 4 * ATTN_WIDTH
    f_hi = f_lo + HEADS
    h = x
    for i in range(depth):
        w = w_in[i]
        w_main = jnp.concatenate([w[:, :f_lo], w[:, f_hi:]], axis=1).astype(BF16)
        w_f = jnp.pad(w[:, f_lo:f_hi], ((0, 0), (0, LANES - HEADS))).astype(BF16)
        bf = jnp.pad(b_f[i], (0, LANES - HEADS)).reshape(1, LANES)
        proj, flog = _in_proj(h.reshape(b * s, d), norm_pre[i].reshape(1, d), w_main, w_f)
        c = _gate_cumsum(flog.reshape(b, s, LANES), bf)
        proj = proj.reshape(b, s, MAIN_COLS)
        g = _attention(proj, c)
        h = _mix_out(g, proj, h, p[i], w_pool[i].astype(BF16), pool_scale[i].reshape(1, POOL_WIDTH),
                     w_out[i].astype(BF16), norm_post[i].reshape(1, d), w_pg[i].astype(BF16),
                     w_pe[i].astype(BF16))
    return h
```

```python
import functools
import math

import jax
import jax.numpy as jnp
import numpy as np
from jax import lax
from jax.experimental import pallas as pl
from jax.experimental.pallas import tpu as pltpu

F32 = jnp.float32
BF16 = jnp.bfloat16

D_MODEL = 1024
PLE_DIM = 256
HEADS = 16
HEAD_DIM = 64
ATTN_WIDTH = HEADS * HEAD_DIM
POOL_WINDOWS = (2, 4, 8, 16)
POOL_GROUP_DIM = 256
POOL_WIDTH = len(POOL_WINDOWS) * POOL_GROUP_DIM
EPS = 1e-6
LOG2E = math.log2(math.e)
Q_SCALE = LOG2E / math.sqrt(HEAD_DIM)

LANES = 128
HALO = 16
MAIN_COLS = 4 * ATTN_WIDTH + 2 * POOL_WIDTH
N_SLABS = MAIN_COLS // LANES
SLABS_PER_COL = ATTN_WIDTH // LANES
PAIRS = HEADS // 2
COL_Q, COL_K, COL_V, COL_ZA, COL_U, COL_ZP = range(6)
BIAS_COLS = 6
BIAS_STRIDE = 8

VMEM_LIMIT = 56 * 1024 * 1024

PROJ_TM = 512
PROJ_CHUNK = 256
ATT_TQ = 256
SCORE_LEAD = 6
SCORE_RING = 16
MIX_TM = 512


def _rms_norm(x, g):
    return x * lax.rsqrt(jnp.mean(x * x, axis=-1, keepdims=True) + EPS) * g


def _silu(z):
    return z * jax.nn.sigmoid(z)


def _bias_lane(head):
    pair, odd = divmod(head, 2)
    return pair * BIAS_STRIDE + (0 if odd else HEAD_DIM)


def _in_proj_kernel(h_ref, g_ref, wa_ref, wb_ref, wf_ref, pj_ref, flog_ref, hn_ref, carry_ref, *,
                    tiles_per_seq):
    tm = h_ref.shape[0]
    seq_tile = lax.rem(pl.program_id(0), tiles_per_seq)
    hn_ref[...] = _rms_norm(h_ref[...], g_ref[...]).astype(BF16)
    flog_ref[...] = jnp.dot(hn_ref[...], wf_ref[...], preferred_element_type=F32)

    @pl.when(seq_tile == 0)
    def _():
        carry_ref[...] = jnp.zeros_like(carry_ref)

    pos = (seq_tile * tm + lax.broadcasted_iota(jnp.int32, (tm, 1), 0) + 1).astype(F32)
    slabs = PROJ_CHUNK // LANES
    n_chunks = MAIN_COLS // PROJ_CHUNK
    first_pool = COL_U * ATTN_WIDTH // PROJ_CHUNK
    pool_chunks = list(range(first_pool, first_pool + len(POOL_WINDOWS)))
    order = [c for c in range(n_chunks) if c not in pool_chunks]
    stride = len(order) // len(pool_chunks)
    for k, c in enumerate(pool_chunks):
        order.insert(k * (stride + 1) + 1, c)
    for c in order:
        w_ref, base = (wa_ref, 0) if c < first_pool else (wb_ref, first_pool)
        cols = slice((c - base) * PROJ_CHUNK, (c - base + 1) * PROJ_CHUNK)
        acc = jnp.dot(hn_ref[...], w_ref[:, cols], preferred_element_type=F32)
        if c * PROJ_CHUNK < ATTN_WIDTH:
            acc = acc * Q_SCALE
        if c in pool_chunks:
            window = POOL_WINDOWS[c - first_pool]
            held = slice((c - first_pool) * PROJ_CHUNK, (c - first_pool + 1) * PROJ_CHUNK)
            win = jnp.concatenate([carry_ref[:, held], acc], axis=0)
            carry_ref[:, held] = acc[tm - HALO:, :]
            span = 1
            while span < window:
                win = win + pltpu.roll(win, span, 0)
                span *= 2
            acc = win[HALO:] * (1.0 / jnp.minimum(pos, float(window))) - acc
        for s in range(slabs):
            pj_ref[c * slabs + s] = acc[:, s * LANES:(s + 1) * LANES].astype(BF16)


def _in_proj(h2d, g, w_a, w_b, w_f, layer, seq):
    t = h2d.shape[0]
    tm = PROJ_TM
    assert seq % tm == 0 and PROJ_CHUNK == POOL_GROUP_DIM
    return pl.pallas_call(
        functools.partial(_in_proj_kernel, tiles_per_seq=seq // tm),
        grid=(t // tm,),
        in_specs=[
            pl.BlockSpec((tm, D_MODEL), lambda i: (i, 0)),
            pl.BlockSpec((None, 1, D_MODEL), lambda i: (layer, 0, 0)),
            pl.BlockSpec((None,) + w_a.shape[1:], lambda i: (layer, 0, 0),
                         pipeline_mode=pl.Buffered(1)),
            pl.BlockSpec((None,) + w_b.shape[1:], lambda i: (layer, 0, 0),
                         pipeline_mode=pl.Buffered(1)),
            pl.BlockSpec((None, D_MODEL, LANES), lambda i: (layer, 0, 0),
                         pipeline_mode=pl.Buffered(1)),
        ],
        out_specs=[
            pl.BlockSpec((N_SLABS, tm, LANES), lambda i: (0, i, 0)),
            pl.BlockSpec((tm, LANES), lambda i: (i, 0)),
        ],
        out_shape=[
            jax.ShapeDtypeStruct((N_SLABS, t, LANES), BF16),
            jax.ShapeDtypeStruct((t, LANES), F32),
        ],
        scratch_shapes=[pltpu.VMEM((tm, D_MODEL), BF16),
                        pltpu.VMEM((HALO, POOL_WIDTH), F32)],
        compiler_params=pltpu.CompilerParams(
            dimension_semantics=("arbitrary",), vmem_limit_bytes=VMEM_LIMIT),
        name="in_proj",
    )(h2d, g, w_a, w_b, w_f)


def _bias_scatter_constants():
    scatter = np.zeros((3, LANES, 2 * LANES), np.float32)
    const = np.zeros((1, 2 * LANES), np.float32)
    for head in range(HEADS):
        lane = _bias_lane(head)
        for piece in range(3):
            scatter[piece, head, lane + piece] = 1.0
            scatter[piece, head, LANES + lane + 3 + piece] = -1.0
        const[0, lane + 3:lane + 6] = 1.0
        const[0, LANES + lane:LANES + lane + 3] = 1.0
    return jnp.asarray(scatter, BF16), jnp.asarray(const, F32)


def _gate_bias_kernel(flog_ref, bf_ref, scat_ref, const_ref, aq_ref, ak_ref):
    x = flog_ref[0] + bf_ref[...]
    c = jnp.minimum(x, 0.0) - jnp.log1p(jnp.exp(-jnp.abs(x)))
    seq = c.shape[0]
    row = lax.broadcasted_iota(jnp.int32, c.shape, 0)
    span = 1
    while span < seq:
        c = c + jnp.where(row >= span, pltpu.roll(c, span, 0), 0.0)
        span *= 2
    c = c * LOG2E
    hi = c.astype(BF16)
    mid = (c - hi.astype(F32)).astype(BF16)
    lo = (c - hi.astype(F32) - mid.astype(F32)).astype(BF16)
    out = (jnp.dot(hi, scat_ref[0], preferred_element_type=F32)
           + jnp.dot(mid, scat_ref[1], preferred_element_type=F32)
           + jnp.dot(lo, scat_ref[2], preferred_element_type=F32)
           + const_ref[...])
    aq_ref[0] = out[:, :LANES].astype(BF16)
    ak_ref[0] = out[:, LANES:].astype(BF16)


def _gate_bias(flog, bf, layer):
    b, s, _ = flog.shape
    scatter, const = _bias_scatter_constants()
    out = jax.ShapeDtypeStruct((b, s, LANES), BF16)
    return pl.pallas_call(
        _gate_bias_kernel,
        grid=(b,),
        in_specs=[
            pl.BlockSpec((1, s, LANES), lambda i: (i, 0, 0)),
            pl.BlockSpec((None, 1, LANES), lambda i: (layer, 0, 0)),
            pl.BlockSpec(scatter.shape, lambda i: (0, 0, 0)),
            pl.BlockSpec(const.shape, lambda i: (0, 0)),
        ],
        out_specs=[pl.BlockSpec((1, s, LANES), lambda i: (i, 0, 0))] * 2,
        out_shape=[out, out],
        compiler_params=pltpu.CompilerParams(
            dimension_semantics=("arbitrary",), vmem_limit_bytes=VMEM_LIMIT),
        name="gate_bias",
    )(flog, bf, scatter, const)


_NT = (((1,), (1,)), ((), ()))


def _fold(x, op):
    out = x[:, :LANES]
    for g in range(1, x.shape[1] // LANES):
        out = op(out, x[:, g * LANES:(g + 1) * LANES])
    return out


def _attn_kernel(q_ref, k_ref, v_ref, z_ref, aq_ref, ak_ref, o_ref,
                 qa_ref, ka_ref, va_ref, s_ref, m_ref):
    seq = q_ref.shape[0]
    tq = ATT_TQ
    pair = pl.program_id(1)
    lane = lax.broadcasted_iota(jnp.int32, (1, LANES), 1)

    for h in range(2):
        first = pair * BIAS_STRIDE + (HEAD_DIM if h == 0 else 0)
        data = (lane < HEAD_DIM) if h == 0 else (lane >= HEAD_DIM)
        dmask = jnp.where(data, 1.0, 0.0).astype(BF16)
        bmask = jnp.where((lane >= first) & (lane < first + BIAS_COLS), 1.0, 0.0).astype(BF16)
        qa_ref[h] = q_ref[...] * dmask + aq_ref[0] * bmask
        ka_ref[h] = k_ref[...] * dmask + ak_ref[0] * bmask
        va_ref[h] = v_ref[...] * dmask + (1.0 - dmask)

    tri = (lax.broadcasted_iota(jnp.int32, (tq, tq), 1)
           <= lax.broadcasted_iota(jnp.int32, (tq, tq), 0))
    lane_t = lax.broadcasted_iota(jnp.int32, (tq, LANES), 1)

    units = [(qi, h) for qi in reversed(range(seq // tq)) for h in range(2)]
    unit_end = np.cumsum([qi + 1 for qi, _ in units])
    n_items = int(unit_end[-1])
    ring = s_ref.shape[0]

    def tile(idx):
        return slice(idx * tq, (idx + 1) * tq)

    def score_stream():
        item = 0
        for u, (qi, h) in enumerate(units):
            qt = qa_ref[h, tile(qi), :]
            m_run = None
            for kc in range(qi + 1):
                s = lax.dot_general(qt, ka_ref[h, tile(kc), :], _NT, preferred_element_type=F32)
                if kc == qi:
                    s = jnp.where(tri, s, -jnp.inf)
                s_ref[item % ring] = s
                fm = _fold(s, jnp.maximum)
                m_run = fm if m_run is None else jnp.maximum(m_run, fm)
                item += 1
                if kc == qi:
                    m_ref[u] = jnp.broadcast_to(jnp.max(m_run, axis=1, keepdims=True),
                                                (tq, LANES))
                yield

    def value_stream():
        item = 0
        outs = {}
        for u, (qi, h) in enumerate(units):
            m = m_ref[u]
            acc = None
            for kc in range(qi + 1):
                s = s_ref[item % ring]
                p = jnp.concatenate(
                    [jnp.exp2(s[:, g * LANES:(g + 1) * LANES] - m) for g in range(tq // LANES)],
                    axis=1).astype(BF16)
                pv = jnp.dot(p, va_ref[h, tile(kc), :], preferred_element_type=F32)
                acc = pv if acc is None else acc + pv
                item += 1
                if kc == qi:
                    outs[h] = acc / pltpu.roll(acc, HEAD_DIM, 1)
                    if h == 1:
                        o = jnp.where(lane_t < HEAD_DIM, outs[0], outs[1])
                        z = z_ref[tile(qi), :].astype(F32)
                        o_ref[tile(qi), :] = (o * _silu(z)).astype(BF16)
                yield

    scores, values = score_stream(), value_stream()
    scored = 0
    for item in range(n_items):
        unit = int(np.searchsorted(unit_end, item, side="right"))
        target = min(n_items, int(unit_end[unit]) + SCORE_LEAD)
        assert target - item <= ring
        while scored < target:
            next(scores)
            scored += 1
        next(values)


def _attention(pj, aq, ak):
    _, b, s, _ = pj.shape

    def slab_spec(col):
        return pl.BlockSpec((None, None, s, LANES),
                            lambda i, j: (col * SLABS_PER_COL + j, i, 0, 0))

    bias_spec = pl.BlockSpec((1, s, LANES), lambda i, j: (i, 0, 0))
    return pl.pallas_call(
        _attn_kernel,
        grid=(b, PAIRS),
        in_specs=[slab_spec(COL_Q), slab_spec(COL_K), slab_spec(COL_V), slab_spec(COL_ZA),
                  bias_spec, bias_spec],
        out_specs=pl.BlockSpec((None, None, s, LANES), lambda i, j: (j, i, 0, 0)),
        out_shape=jax.ShapeDtypeStruct((PAIRS, b, s, LANES), BF16),
        scratch_shapes=[
            pltpu.VMEM((2, s, LANES), BF16),
            pltpu.VMEM((2, s, LANES), BF16),
            pltpu.VMEM((2, s, LANES), BF16),
            pltpu.VMEM((SCORE_RING, ATT_TQ, ATT_TQ), F32),
            pltpu.VMEM((2 * (s // ATT_TQ), ATT_TQ, LANES), F32),
        ],
        compiler_params=pltpu.CompilerParams(
            dimension_semantics=("arbitrary", "arbitrary"), vmem_limit_bytes=VMEM_LIMIT),
        name="fox_attn",
    )(pj, pj, pj, pj, aq, ak)


def _mix_kernel(g_ref, pooled_ref, zp_ref, h_ref, p_ref, wpool_ref, ps_ref, wout_ref,
                gpost_ref, wpg_ref, wpe_ref, out_ref):
    slabs_per_group = POOL_GROUP_DIM // LANES
    mixed = []
    for g in range(len(POOL_WINDOWS)):
        lhs = jnp.concatenate([pooled_ref[sl] for sl in range(g * slabs_per_group,
                                                              (g + 1) * slabs_per_group)], axis=1)
        mixed.append(jnp.dot(lhs, wpool_ref[g], preferred_element_type=F32))
    zp = jnp.concatenate([zp_ref[sl] for sl in range(SLABS_PER_COL)], axis=1).astype(F32)
    pool_out = (jnp.concatenate(mixed, axis=1) * ps_ref[...] * _silu(zp)).astype(BF16)

    attn_out = jnp.concatenate([g_ref[sl] for sl in range(PAIRS)], axis=1)
    mix = jnp.dot(attn_out, wout_ref[:ATTN_WIDTH, :], preferred_element_type=F32)
    mix = mix + jnp.dot(pool_out, wout_ref[ATTN_WIDTH:, :], preferred_element_type=F32)
    h1 = h_ref[...] + _rms_norm(mix, gpost_ref[...])
    gate = jax.nn.sigmoid(jnp.dot(h1.astype(BF16), wpg_ref[...], preferred_element_type=F32))
    pe = jnp.dot(p_ref[...].astype(BF16), wpe_ref[...], preferred_element_type=F32)
    out_ref[...] = h1 + gate * pe


def _mix_out(g, pj, h, p, w_pool, pool_scale, w_out, g_post, w_pg, w_pe, layer):
    b, s, _ = h.shape
    tm = MIX_TM

    def resident(arr):
        block = (None,) + arr.shape[1:]
        return pl.BlockSpec(block, lambda i, j: (layer,) + (0,) * (arr.ndim - 1),
                            pipeline_mode=pl.Buffered(1))

    def col_spec(col):
        return pl.BlockSpec((SLABS_PER_COL, None, tm, LANES), lambda i, j: (col, i, j, 0))

    return pl.pallas_call(
        _mix_kernel,
        grid=(b, s // tm),
        in_specs=[
            col_spec(0), col_spec(COL_U), col_spec(COL_ZP),
            pl.BlockSpec((None, tm, D_MODEL), lambda i, j: (i, j, 0)),
            pl.BlockSpec((None, None, tm, PLE_DIM), lambda i, j: (layer, i, j, 0)),
            resident(w_pool), resident(pool_scale), resident(w_out), resident(g_post),
            resident(w_pg), resident(w_pe),
        ],
        out_specs=pl.BlockSpec((None, tm, D_MODEL), lambda i, j: (i, j, 0)),
        out_shape=jax.ShapeDtypeStruct((b, s, D_MODEL), F32),
        compiler_params=pltpu.CompilerParams(
            dimension_semantics=("arbitrary", "arbitrary"), vmem_limit_bytes=VMEM_LIMIT),
        name="mix_out",
    )(g, pj, pj, h, p, w_pool, pool_scale, w_out, g_post, w_pg, w_pe)


def kernel(x, p, norm_pre, norm_post, w_in, b_f, w_pool, pool_scale, w_out, w_pg, w_pe):
    b, s, d = x.shape
    depth = w_in.shape[0]
    f_lo = 4 * ATTN_WIDTH
    f_hi = f_lo + HEADS
    lane_pad = LANES - HEADS
    w_a = w_in[:, :, :f_lo].astype(BF16)
    w_b = w_in[:, :, f_hi:].astype(BF16)
    w_f = jnp.pad(w_in[:, :, f_lo:f_hi], ((0, 0), (0, 0), (0, lane_pad))).astype(BF16)
    bf = jnp.pad(b_f, ((0, 0), (0, lane_pad))).reshape(depth, 1, LANES)
    g_pre = norm_pre.reshape(depth, 1, d)
    g_post = norm_post.reshape(depth, 1, d)
    ps = pool_scale.reshape(depth, 1, POOL_WIDTH)
    w_pool, w_out, w_pg, w_pe = (w.astype(BF16) for w in (w_pool, w_out, w_pg, w_pe))

    h = x
    for layer in range(depth):
        pj, flog = _in_proj(h.reshape(b * s, d), g_pre, w_a, w_b, w_f, layer, s)
        aq, ak = _gate_bias(flog.reshape(b, s, LANES), bf, layer)
        pj = pj.reshape(N_SLABS, b, s, LANES)
        g = _attention(pj, aq, ak)
        h = _mix_out(g, pj, h, p, w_pool, ps, w_out, g_post, w_pg, w_pe, layer)
    return h
```

```python
import functools
import math

import jax
import jax.numpy as jnp
import numpy as np
from jax import lax
from jax.experimental import pallas as pl
from jax.experimental.pallas import tpu as pltpu

F32 = jnp.float32
BF16 = jnp.bfloat16

D_MODEL = 1024
PLE_DIM = 256
HEADS = 16
HEAD_DIM = 64
ATTN_WIDTH = HEADS * HEAD_DIM
POOL_WINDOWS = (2, 4, 8, 16)
POOL_GROUP_DIM = 256
POOL_WIDTH = len(POOL_WINDOWS) * POOL_GROUP_DIM
EPS = 1e-6
LOG2E = math.log2(math.e)
Q_SCALE = LOG2E / math.sqrt(HEAD_DIM)

LANES = 128
SUBLANES = 8
HALO = 16
MAIN_COLS = 4 * ATTN_WIDTH + 2 * POOL_WIDTH
N_SLABS = MAIN_COLS // LANES
SLABS_PER_COL = ATTN_WIDTH // LANES
PAIRS = HEADS // 2
COL_Q, COL_K, COL_V, COL_ZA, COL_U, COL_ZP = range(6)
BIAS_COLS = 6
BIAS_STRIDE = 8

VMEM_LIMIT = 56 * 1024 * 1024

PROJ_TM = 512
PROJ_CHUNK = 256
ATT_TQ = 256
SCORE_LEAD = 6
SCORE_RING = 16
VT_ROWS = HEAD_DIM + 16
MIX_TM = 512


def _rms_norm(x, g):
    return x * lax.rsqrt(jnp.mean(x * x, axis=-1, keepdims=True) + EPS) * g


def _silu(z):
    return z * jax.nn.sigmoid(z)


def _bias_lane(head):
    pair, odd = divmod(head, 2)
    return pair * BIAS_STRIDE + (0 if odd else HEAD_DIM)


def _in_proj_kernel(h_ref, g_ref, wa_ref, wb_ref, wf_ref, pj_ref, flog_ref, hn_ref, carry_ref, *,
                    tiles_per_seq):
    tm = h_ref.shape[0]
    seq_tile = lax.rem(pl.program_id(0), tiles_per_seq)
    hn_ref[...] = _rms_norm(h_ref[...], g_ref[...]).astype(BF16)
    flog_ref[...] = jnp.dot(hn_ref[...], wf_ref[...], preferred_element_type=F32)

    @pl.when(seq_tile == 0)
    def _():
        carry_ref[...] = jnp.zeros_like(carry_ref)

    pos = (seq_tile * tm + lax.broadcasted_iota(jnp.int32, (tm, 1), 0) + 1).astype(F32)
    slabs = PROJ_CHUNK // LANES
    n_chunks = MAIN_COLS // PROJ_CHUNK
    first_pool = COL_U * ATTN_WIDTH // PROJ_CHUNK
    pool_chunks = list(range(first_pool, first_pool + len(POOL_WINDOWS)))
    order = [c for c in range(n_chunks) if c not in pool_chunks]
    stride = len(order) // len(pool_chunks)
    for k, c in enumerate(pool_chunks):
        order.insert(k * (stride + 1) + 1, c)
    for c in order:
        w_ref, base = (wa_ref, 0) if c < first_pool else (wb_ref, first_pool)
        cols = slice((c - base) * PROJ_CHUNK, (c - base + 1) * PROJ_CHUNK)
        acc = jnp.dot(hn_ref[...], w_ref[:, cols], preferred_element_type=F32)
        if c * PROJ_CHUNK < ATTN_WIDTH:
            acc = acc * Q_SCALE
        if c in pool_chunks:
            window = POOL_WINDOWS[c - first_pool]
            held = slice((c - first_pool) * PROJ_CHUNK, (c - first_pool + 1) * PROJ_CHUNK)
            win = jnp.concatenate([carry_ref[:, held], acc], axis=0)
            carry_ref[:, held] = acc[tm - HALO:, :]
            span = 1
            while span < window:
                win = win + pltpu.roll(win, span, 0)
                span *= 2
            acc = win[HALO:] * (1.0 / jnp.minimum(pos, float(window))) - acc
        for s in range(slabs):
            pj_ref[c * slabs + s] = acc[:, s * LANES:(s + 1) * LANES].astype(BF16)


def _in_proj(h2d, g, w_a, w_b, w_f, layer, seq):
    t = h2d.shape[0]
    tm = PROJ_TM
    assert seq % tm == 0 and PROJ_CHUNK == POOL_GROUP_DIM
    return pl.pallas_call(
        functools.partial(_in_proj_kernel, tiles_per_seq=seq // tm),
        grid=(t // tm,),
        in_specs=[
            pl.BlockSpec((tm, D_MODEL), lambda i: (i, 0)),
            pl.BlockSpec((None, 1, D_MODEL), lambda i: (layer, 0, 0)),
            pl.BlockSpec((None,) + w_a.shape[1:], lambda i: (layer, 0, 0),
                         pipeline_mode=pl.Buffered(1)),
            pl.BlockSpec((None,) + w_b.shape[1:], lambda i: (layer, 0, 0),
                         pipeline_mode=pl.Buffered(1)),
            pl.BlockSpec((None, D_MODEL, LANES), lambda i: (layer, 0, 0),
                         pipeline_mode=pl.Buffered(1)),
        ],
        out_specs=[
            pl.BlockSpec((N_SLABS, tm, LANES), lambda i: (0, i, 0)),
            pl.BlockSpec((tm, LANES), lambda i: (i, 0)),
        ],
        out_shape=[
            jax.ShapeDtypeStruct((N_SLABS, t, LANES), BF16),
            jax.ShapeDtypeStruct((t, LANES), F32),
        ],
        scratch_shapes=[pltpu.VMEM((tm, D_MODEL), BF16),
                        pltpu.VMEM((HALO, POOL_WIDTH), F32)],
        compiler_params=pltpu.CompilerParams(
            dimension_semantics=("arbitrary",), vmem_limit_bytes=VMEM_LIMIT),
        name="in_proj",
    )(h2d, g, w_a, w_b, w_f)


def _bias_scatter_constants():
    scatter = np.zeros((3, LANES, 2 * LANES), np.float32)
    const = np.zeros((1, 2 * LANES), np.float32)
    for head in range(HEADS):
        lane = _bias_lane(head)
        for piece in range(3):
            scatter[piece, head, lane + piece] = 1.0
            scatter[piece, head, LANES + lane + 3 + piece] = -1.0
        const[0, lane + 3:lane + 6] = 1.0
        const[0, LANES + lane:LANES + lane + 3] = 1.0
    return jnp.asarray(scatter, BF16), jnp.asarray(const, F32)


def _gate_bias_kernel(flog_ref, bf_ref, scat_ref, const_ref, aq_ref, ak_ref):
    x = flog_ref[0] + bf_ref[...]
    c = jnp.minimum(x, 0.0) - jnp.log1p(jnp.exp(-jnp.abs(x)))
    seq = c.shape[0]
    row = lax.broadcasted_iota(jnp.int32, c.shape, 0)
    span = 1
    while span < seq:
        c = c + jnp.where(row >= span, pltpu.roll(c, span, 0), 0.0)
        span *= 2
    c = c * LOG2E
    hi = c.astype(BF16)
    mid = (c - hi.astype(F32)).astype(BF16)
    lo = (c - hi.astype(F32) - mid.astype(F32)).astype(BF16)
    out = (jnp.dot(hi, scat_ref[0], preferred_element_type=F32)
           + jnp.dot(mid, scat_ref[1], preferred_element_type=F32)
           + jnp.dot(lo, scat_ref[2], preferred_element_type=F32)
           + const_ref[...])
    aq_ref[0] = out[:, :LANES].astype(BF16)
    ak_ref[0] = out[:, LANES:].astype(BF16)


def _gate_bias(flog, bf, layer):
    b, s, _ = flog.shape
    scatter, const = _bias_scatter_constants()
    out = jax.ShapeDtypeStruct((b, s, LANES), BF16)
    return pl.pallas_call(
        _gate_bias_kernel,
        grid=(b,),
        in_specs=[
            pl.BlockSpec((1, s, LANES), lambda i: (i, 0, 0)),
            pl.BlockSpec((None, 1, LANES), lambda i: (layer, 0, 0)),
            pl.BlockSpec(scatter.shape, lambda i: (0, 0, 0)),
            pl.BlockSpec(const.shape, lambda i: (0, 0)),
        ],
        out_specs=[pl.BlockSpec((1, s, LANES), lambda i: (i, 0, 0))] * 2,
        out_shape=[out, out],
        compiler_params=pltpu.CompilerParams(
            dimension_semantics=("arbitrary",), vmem_limit_bytes=VMEM_LIMIT),
        name="gate_bias",
    )(flog, bf, scatter, const)


_NT = (((1,), (1,)), ((), ()))


def _fold(x, op):
    out = x[:, :LANES]
    for g in range(1, x.shape[1] // LANES):
        out = op(out, x[:, g * LANES:(g + 1) * LANES])
    return out


def _attn_kernel(q_ref, k_ref, v_ref, z_ref, aq_ref, ak_ref, o_ref,
                 qa_ref, ka_ref, vt_ref, s_ref, m_ref):
    seq = q_ref.shape[0]
    tq = ATT_TQ
    pair = pl.program_id(1)
    lane = lax.broadcasted_iota(jnp.int32, (1, LANES), 1)

    for h in range(2):
        first = pair * BIAS_STRIDE + (HEAD_DIM if h == 0 else 0)
        data = (lane < HEAD_DIM) if h == 0 else (lane >= HEAD_DIM)
        dmask = jnp.where(data, 1.0, 0.0).astype(BF16)
        bmask = jnp.where((lane >= first) & (lane < first + BIAS_COLS), 1.0, 0.0).astype(BF16)
        qa_ref[h] = q_ref[...] * dmask + aq_ref[0] * bmask
        ka_ref[h] = k_ref[...] * dmask + ak_ref[0] * bmask
    vt = v_ref[...].astype(F32).T.astype(BF16)
    ones = jnp.ones((VT_ROWS - HEAD_DIM, seq), BF16)
    for h in range(2):
        vt_ref[h] = jnp.concatenate([vt[h * HEAD_DIM:(h + 1) * HEAD_DIM], ones], axis=0)

    causal = (lax.broadcasted_iota(jnp.int32, (tq, tq), 0)
              <= lax.broadcasted_iota(jnp.int32, (tq, tq), 1))

    units = [(qi, h) for qi in reversed(range(seq // tq)) for h in range(2)]
    unit_end = np.cumsum([qi + 1 for qi, _ in units])
    n_items = int(unit_end[-1])
    ring = s_ref.shape[0]

    def tile(idx):
        return slice(idx * tq, (idx + 1) * tq)

    def score_stream():
        item = 0
        for u, (qi, h) in enumerate(units):
            qt = qa_ref[h, tile(qi), :]
            m_run = None
            for kc in range(qi + 1):
                st = lax.dot_general(ka_ref[h, tile(kc), :], qt, _NT, preferred_element_type=F32)
                if kc == qi:
                    st = jnp.where(causal, st, -jnp.inf)
                s_ref[item % ring] = st
                part = jnp.max(st.reshape(tq // SUBLANES, SUBLANES, tq), axis=0)
                m_run = part if m_run is None else jnp.maximum(m_run, part)
                item += 1
                if kc == qi:
                    m_ref[u] = jnp.broadcast_to(jnp.max(m_run, axis=0, keepdims=True),
                                                (SUBLANES, tq))
                yield

    def value_stream():
        item = 0
        outs = {}
        for u, (qi, h) in enumerate(units):
            m = m_ref[u]
            acc = None
            for kc in range(qi + 1):
                st = s_ref[item % ring].reshape(tq // SUBLANES, SUBLANES, tq)
                pt = jnp.exp2(st - m).reshape(tq, tq).astype(BF16)
                pv = jnp.dot(vt_ref[h, :, tile(kc)], pt, preferred_element_type=F32)
                acc = pv if acc is None else acc + pv
                item += 1
                if kc == qi:
                    outs[h] = acc[:HEAD_DIM] / acc[HEAD_DIM:HEAD_DIM + 1]
                    if h == 1:
                        o = jnp.concatenate([outs[0], outs[1]], axis=0).T
                        z = z_ref[tile(qi), :].astype(F32)
                        o_ref[tile(qi), :] = (o * _silu(z)).astype(BF16)
                yield

    scores, values = score_stream(), value_stream()
    scored = 0
    for item in range(n_items):
        unit = int(np.searchsorted(unit_end, item, side="right"))
        target = min(n_items, int(unit_end[unit]) + SCORE_LEAD)
        assert target - item <= ring
        while scored < target:
            next(scores)
            scored += 1
        next(values)


def _attention(pj, aq, ak):
    _, b, s, _ = pj.shape

    def slab_spec(col):
        return pl.BlockSpec((None, None, s, LANES),
                            lambda i, j: (col * SLABS_PER_COL + j, i, 0, 0))

    bias_spec = pl.BlockSpec((1, s, LANES), lambda i, j: (i, 0, 0))
    return pl.pallas_call(
        _attn_kernel,
        grid=(b, PAIRS),
        in_specs=[slab_spec(COL_Q), slab_spec(COL_K), slab_spec(COL_V), slab_spec(COL_ZA),
                  bias_spec, bias_spec],
        out_specs=pl.BlockSpec((None, None, s, LANES), lambda i, j: (j, i, 0, 0)),
        out_shape=jax.ShapeDtypeStruct((PAIRS, b, s, LANES), BF16),
        scratch_shapes=[
            pltpu.VMEM((2, s, LANES), BF16),
            pltpu.VMEM((2, s, LANES), BF16),
            pltpu.VMEM((2, VT_ROWS, s), BF16),
            pltpu.VMEM((SCORE_RING, ATT_TQ, ATT_TQ), F32),
            pltpu.VMEM((2 * (s // ATT_TQ), SUBLANES, ATT_TQ), F32),
        ],
        compiler_params=pltpu.CompilerParams(
            dimension_semantics=("arbitrary", "arbitrary"), vmem_limit_bytes=VMEM_LIMIT),
        name="fox_attn",
    )(pj, pj, pj, pj, aq, ak)


def _mix_kernel(g_ref, pooled_ref, zp_ref, h_ref, p_ref, wpool_ref, ps_ref, wout_ref,
                gpost_ref, wpg_ref, wpe_ref, out_ref):
    slabs_per_group = POOL_GROUP_DIM // LANES
    mixed = []
    for g in range(len(POOL_WINDOWS)):
        lhs = jnp.concatenate([pooled_ref[sl] for sl in range(g * slabs_per_group,
                                                              (g + 1) * slabs_per_group)], axis=1)
        mixed.append(jnp.dot(lhs, wpool_ref[g], preferred_element_type=F32))
    zp = jnp.concatenate([zp_ref[sl] for sl in range(SLABS_PER_COL)], axis=1).astype(F32)
    pool_out = (jnp.concatenate(mixed, axis=1) * ps_ref[...] * _silu(zp)).astype(BF16)

    attn_out = jnp.concatenate([g_ref[sl] for sl in range(PAIRS)], axis=1)
    mix = jnp.dot(attn_out, wout_ref[:ATTN_WIDTH, :], preferred_element_type=F32)
    mix = mix + jnp.dot(pool_out, wout_ref[ATTN_WIDTH:, :], preferred_element_type=F32)
    h1 = h_ref[...] + _rms_norm(mix, gpost_ref[...])
    gate = jax.nn.sigmoid(jnp.dot(h1.astype(BF16), wpg_ref[...], preferred_element_type=F32))
    pe = jnp.dot(p_ref[...].astype(BF16), wpe_ref[...], preferred_element_type=F32)
    out_ref[...] = h1 + gate * pe


def _mix_out(g, pj, h, p, w_pool, pool_scale, w_out, g_post, w_pg, w_pe, layer):
    b, s, _ = h.shape
    tm = MIX_TM

    def resident(arr):
        block = (None,) + arr.shape[1:]
        return pl.BlockSpec(block, lambda i, j: (layer,) + (0,) * (arr.ndim - 1),
                            pipeline_mode=pl.Buffered(1))

    def col_spec(col):
        return pl.BlockSpec((SLABS_PER_COL, None, tm, LANES), lambda i, j: (col, i, j, 0))

    return pl.pallas_call(
        _mix_kernel,
        grid=(b, s // tm),
        in_specs=[
            col_spec(0), col_spec(COL_U), col_spec(COL_ZP),
            pl.BlockSpec((None, tm, D_MODEL), lambda i, j: (i, j, 0)),
            pl.BlockSpec((None, None, tm, PLE_DIM), lambda i, j: (layer, i, j, 0)),
            resident(w_pool), resident(pool_scale), resident(w_out), resident(g_post),
            resident(w_pg), resident(w_pe),
        ],
        out_specs=pl.BlockSpec((None, tm, D_MODEL), lambda i, j: (i, j, 0)),
        out_shape=jax.ShapeDtypeStruct((b, s, D_MODEL), F32),
        compiler_params=pltpu.CompilerParams(
            dimension_semantics=("arbitrary", "arbitrary"), vmem_limit_bytes=VMEM_LIMIT),
        name="mix_out",
    )(g, pj, pj, h, p, w_pool, pool_scale, w_out, g_post, w_pg, w_pe)


def kernel(x, p, norm_pre, norm_post, w_in, b_f, w_pool, pool_scale, w_out, w_pg, w_pe):
    b, s, d = x.shape
    depth = w_in.shape[0]
    f_lo = 4 * ATTN_WIDTH
    f_hi = f_lo + HEADS
    lane_pad = LANES - HEADS
    w_a = w_in[:, :, :f_lo].astype(BF16)
    w_b = w_in[:, :, f_hi:].astype(BF16)
    w_f = jnp.pad(w_in[:, :, f_lo:f_hi], ((0, 0), (0, 0), (0, lane_pad))).astype(BF16)
    bf = jnp.pad(b_f, ((0, 0), (0, lane_pad))).reshape(depth, 1, LANES)
    g_pre = norm_pre.reshape(depth, 1, d)
    g_post = norm_post.reshape(depth, 1, d)
    ps = pool_scale.reshape(depth, 1, POOL_WIDTH)
    w_pool, w_out, w_pg, w_pe = (w.astype(BF16) for w in (w_pool, w_out, w_pg, w_pe))

    h = x
    for layer in range(depth):
        pj, flog = _in_proj(h.reshape(b * s, d), g_pre, w_a, w_b, w_f, layer, s)
        aq, ak = _gate_bias(flog.reshape(b, s, LANES), bf, layer)
        pj = pj.reshape(N_SLABS, b, s, LANES)
        g = _attention(pj, aq, ak)
        h = _mix_out(g, pj, h, p, w_pool, ps, w_out, g_post, w_pg, w_pe, layer)
    return h
```

```python
import functools
import math

import jax
import jax.numpy as jnp
import numpy as np
from jax import lax
from jax.experimental import pallas as pl
from jax.experimental.pallas import tpu as pltpu

F32 = jnp.float32
BF16 = jnp.bfloat16

D_MODEL = 1024
PLE_DIM = 256
HEADS = 16
HEAD_DIM = 64
ATTN_WIDTH = HEADS * HEAD_DIM
POOL_WINDOWS = (2, 4, 8, 16)
POOL_GROUP_DIM = 256
POOL_WIDTH = len(POOL_WINDOWS) * POOL_GROUP_DIM
EPS = 1e-6
LOG2E = math.log2(math.e)
Q_SCALE = LOG2E / math.sqrt(HEAD_DIM)

LANES = 128
HALO = 16
MAIN_COLS = 4 * ATTN_WIDTH + 2 * POOL_WIDTH
N_SLABS = MAIN_COLS // LANES
SLABS_PER_COL = ATTN_WIDTH // LANES
PAIRS = HEADS // 2
COL_Q, COL_K, COL_V, COL_ZA, COL_U, COL_ZP = range(6)
BIAS_COLS = 6
BIAS_STRIDE = 8

VMEM_LIMIT = 56 * 1024 * 1024

PROJ_TM = 512
PROJ_CHUNK = 256
ATT_TQ = 256
ATT_SLABS = 2
SCORE_LEAD = 6
SCORE_RING = 16
MIX_TM = 512


def _rms_norm(x, g):
    return x * lax.rsqrt(jnp.mean(x * x, axis=-1, keepdims=True) + EPS) * g


def _silu(z):
    return z * jax.nn.sigmoid(z)


def _bias_lane(head):
    pair, odd = divmod(head, 2)
    return pair * BIAS_STRIDE + (0 if odd else HEAD_DIM)


def _bias_scatter_constants():
    scatter = np.zeros((3, LANES, 2 * LANES), np.float32)
    const = np.zeros((1, 2 * LANES), np.float32)
    for head in range(HEADS):
        lane = _bias_lane(head)
        for piece in range(3):
            scatter[piece, head, lane + piece] = 1.0
            scatter[piece, head, LANES + lane + 3 + piece] = -1.0
        const[0, lane + 3:lane + 6] = 1.0
        const[0, LANES + lane:LANES + lane + 3] = 1.0
    return jnp.asarray(scatter, BF16), jnp.asarray(const, F32)


def _in_proj_kernel(h_ref, g_ref, wa_ref, wb_ref, wf_ref, bf_ref, scat_ref, const_ref,
                    pj_ref, aq_ref, ak_ref, hn_ref, carry_ref, gate_ref, *, tiles_per_seq):
    tm = h_ref.shape[0]
    seq_tile = lax.rem(pl.program_id(0), tiles_per_seq)
    hn_ref[...] = _rms_norm(h_ref[...], g_ref[...]).astype(BF16)

    @pl.when(seq_tile == 0)
    def _():
        carry_ref[...] = jnp.zeros_like(carry_ref)
        gate_ref[...] = jnp.zeros_like(gate_ref)

    def gate_bias_stream():
        x = jnp.dot(hn_ref[...], wf_ref[...], preferred_element_type=F32) + bf_ref[...]
        c = jnp.minimum(x, 0.0) - jnp.log1p(jnp.exp(-jnp.abs(x)))
        yield
        row = lax.broadcasted_iota(jnp.int32, c.shape, 0)
        span = 1
        while span < tm:
            c = c + jnp.where(row >= span, pltpu.roll(c, span, 0), 0.0)
            span *= 2
            yield
        c = c + gate_ref[0:1, :]
        gate_ref[...] = jnp.broadcast_to(c[tm - 1:tm, :], gate_ref.shape)
        c = c * LOG2E
        hi = c.astype(BF16)
        mid = (c - hi.astype(F32)).astype(BF16)
        lo = (c - hi.astype(F32) - mid.astype(F32)).astype(BF16)
        yield
        bias = (jnp.dot(hi, scat_ref[0], preferred_element_type=F32)
                + jnp.dot(mid, scat_ref[1], preferred_element_type=F32)
                + jnp.dot(lo, scat_ref[2], preferred_element_type=F32)
                + const_ref[...])
        aq_ref[...] = bias[:, :LANES].astype(BF16)
        ak_ref[...] = bias[:, LANES:].astype(BF16)
        yield

    gate_steps = gate_bias_stream()
    pos =(seq_tile * tm + lax.broadcasted_iota(jnp.int32, (tm, 1), 0) + 1).astype(F32)
    slabs = PROJ_CHUNK // LANES
    n_chunks = MAIN_COLS // PROJ_CHUNK
    first_pool = COL_U * ATTN_WIDTH // PROJ_CHUNK
    pool_chunks = list(range(first_pool, first_pool + len(POOL_WINDOWS)))
    order = [c for c in range(n_chunks) if c not in pool_chunks]
    stride = len(order) // len(pool_chunks)
    for k, c in enumerate(pool_chunks):
        order.insert(k * (stride + 1) + 1, c)
    for c in order:
        w_ref, base = (wa_ref, 0) if c < first_pool else (wb_ref, first_pool)
        cols = slice((c - base) * PROJ_CHUNK, (c - base + 1) * PROJ_CHUNK)
        acc = jnp.dot(hn_ref[...], w_ref[:, cols], preferred_element_type=F32)
        if c * PROJ_CHUNK < ATTN_WIDTH:
            acc = acc * Q_SCALE
        if c in pool_chunks:
            window = POOL_WINDOWS[c - first_pool]
            held = slice((c - first_pool) * PROJ_CHUNK, (c - first_pool + 1) * PROJ_CHUNK)
            win = jnp.concatenate([carry_ref[:, held], acc], axis=0)
            carry_ref[:, held] = acc[tm - HALO:, :]
            span = 1
            while span < window:
                win = win + pltpu.roll(win, span, 0)
                span *= 2
            acc = win[HALO:] * (1.0 / jnp.minimum(pos, float(window))) - acc
        for s in range(slabs):
            pj_ref[c * slabs + s] = acc[:, s * LANES:(s + 1) * LANES].astype(BF16)
        next(gate_steps, None)
    for _ in gate_steps:
        pass


def _in_proj(h2d, g, w_a, w_b, w_f, bf, layer, seq):
    t = h2d.shape[0]
    tm = PROJ_TM
    assert seq % tm == 0 and PROJ_CHUNK == POOL_GROUP_DIM
    scatter, const = _bias_scatter_constants()
    bias_out = jax.ShapeDtypeStruct((t, LANES), BF16)
    return pl.pallas_call(
        functools.partial(_in_proj_kernel, tiles_per_seq=seq // tm),
        grid=(t // tm,),
        in_specs=[
            pl.BlockSpec((tm, D_MODEL), lambda i: (i, 0)),
            pl.BlockSpec((None, 1, D_MODEL), lambda i: (layer, 0, 0)),
            pl.BlockSpec((None,) + w_a.shape[1:], lambda i: (layer, 0, 0),
                         pipeline_mode=pl.Buffered(1)),
            pl.BlockSpec((None,) + w_b.shape[1:], lambda i: (layer, 0, 0),
                         pipeline_mode=pl.Buffered(1)),
            pl.BlockSpec((None, D_MODEL, LANES), lambda i: (layer, 0, 0),
                         pipeline_mode=pl.Buffered(1)),
            pl.BlockSpec((None, 1, LANES), lambda i: (layer, 0, 0)),
            pl.BlockSpec(scatter.shape, lambda i: (0, 0, 0)),
            pl.BlockSpec(const.shape, lambda i: (0, 0)),
        ],
        out_specs=[
            pl.BlockSpec((N_SLABS, tm, LANES), lambda i: (0, i, 0)),
            pl.BlockSpec((tm, LANES), lambda i: (i, 0)),
            pl.BlockSpec((tm, LANES), lambda i: (i, 0)),
        ],
        out_shape=[jax.ShapeDtypeStruct((N_SLABS, t, LANES), BF16), bias_out, bias_out],
        scratch_shapes=[pltpu.VMEM((tm, D_MODEL), BF16),
                        pltpu.VMEM((HALO, POOL_WIDTH), F32),
                        pltpu.VMEM((8, LANES), F32)],
        compiler_params=pltpu.CompilerParams(
            dimension_semantics=("arbitrary",), vmem_limit_bytes=VMEM_LIMIT),
        name="in_proj",
    )(h2d, g, w_a, w_b, w_f, bf, scatter, const)


_NT = (((1,), (1,)), ((), ()))


def _fold(x, op):
    out = x[:, :LANES]
    for g in range(1, x.shape[1] // LANES):
        out = op(out, x[:, g * LANES:(g + 1) * LANES])
    return out


def _attn_kernel(q_ref, k_ref, v_ref, z_ref, aq_ref, ak_ref, o_ref,
                 qa_ref, ka_ref, va_ref, s_ref, m_ref):
    n_slabs, seq, _ = q_ref.shape
    n_heads = 2 * n_slabs
    tq = ATT_TQ
    lane = lax.broadcasted_iota(jnp.int32, (1, LANES), 1)

    for hh in range(n_heads):
        sl, h = divmod(hh, 2)
        pair = pl.program_id(1) * n_slabs + sl
        first = pair * BIAS_STRIDE + (HEAD_DIM if h == 0 else 0)
        data = (lane < HEAD_DIM) if h == 0 else (lane >= HEAD_DIM)
        dmask = jnp.where(data, 1.0, 0.0).astype(BF16)
        bmask = jnp.where((lane >= first) & (lane < first + BIAS_COLS), 1.0, 0.0).astype(BF16)
        qa_ref[hh] = q_ref[sl] * dmask + aq_ref[0] * bmask
        ka_ref[hh] = k_ref[sl] * dmask + ak_ref[0] * bmask
        va_ref[hh] = v_ref[sl] * dmask + (1.0 - dmask)

    tri = (lax.broadcasted_iota(jnp.int32, (tq, tq), 1)
           <= lax.broadcasted_iota(jnp.int32, (tq, tq), 0))
    lane_t = lax.broadcasted_iota(jnp.int32, (tq, LANES), 1)

    units = [(qi, hh) for qi in reversed(range(seq // tq)) for hh in range(n_heads)]
    unit_end = np.cumsum([qi + 1 for qi, _ in units])
    n_items = int(unit_end[-1])
    ring = s_ref.shape[0]

    def tile(idx):
        return slice(idx * tq, (idx + 1) * tq)

    def score_stream():
        item = 0
        for u, (qi, h) in enumerate(units):
            qt = qa_ref[h, tile(qi), :]
            m_run = None
            for kc in range(qi + 1):
                s = lax.dot_general(qt, ka_ref[h, tile(kc), :], _NT, preferred_element_type=F32)
                if kc == qi:
                    s = jnp.where(tri, s, -jnp.inf)
                s_ref[item % ring] = s
                fm = _fold(s, jnp.maximum)
                m_run = fm if m_run is None else jnp.maximum(m_run, fm)
                item += 1
                if kc == qi:
                    m_ref[u] = jnp.broadcast_to(jnp.max(m_run, axis=1, keepdims=True),
                                                (tq, LANES))
                yield

    def value_stream():
        item = 0
        outs = {}
        for u, (qi, h) in enumerate(units):
            m = m_ref[u]
            acc = None
            for kc in range(qi + 1):
                s = s_ref[item % ring]
                p = jnp.concatenate(
                    [jnp.exp2(s[:, g * LANES:(g + 1) * LANES] - m) for g in range(tq // LANES)],
                    axis=1).astype(BF16)
                pv = jnp.dot(p, va_ref[h, tile(kc), :], preferred_element_type=F32)
                acc = pv if acc is None else acc + pv
                item += 1
                if kc == qi:
                    outs[h] = acc / pltpu.roll(acc, HEAD_DIM, 1)
                    if h % 2 == 1:
                        o = jnp.where(lane_t < HEAD_DIM, outs[h - 1], outs[h])
                        z = z_ref[h // 2, tile(qi), :].astype(F32)
                        o_ref[h // 2, tile(qi), :] = (o * _silu(z)).astype(BF16)
                yield

    scores, values = score_stream(), value_stream()
    scored = 0
    for item in range(n_items):
        unit = int(np.searchsorted(unit_end, item, side="right"))
        target = min(n_items, int(unit_end[unit]) + SCORE_LEAD)
        assert target - item <= ring
        while scored < target:
            next(scores)
            scored += 1
        next(values)


def _attention(pj, aq, ak):
    _, b, s, _ = pj.shape
    n = ATT_SLABS
    heads = 2 * n

    def slab_spec(col):
        return pl.BlockSpec((n, None, s, LANES),
                            lambda i, j: (col * (SLABS_PER_COL // n) + j, i, 0, 0))

    bias_spec = pl.BlockSpec((1, s, LANES), lambda i, j: (i, 0, 0))
    return pl.pallas_call(
        _attn_kernel,
        grid=(b, PAIRS // n),
        in_specs=[slab_spec(COL_Q), slab_spec(COL_K), slab_spec(COL_V), slab_spec(COL_ZA),
                  bias_spec, bias_spec],
        out_specs=pl.BlockSpec((n, None, s, LANES), lambda i, j: (j, i, 0, 0)),
        out_shape=jax.ShapeDtypeStruct((PAIRS, b, s, LANES), BF16),
        scratch_shapes=[
            pltpu.VMEM((heads, s, LANES), BF16),
            pltpu.VMEM((heads, s, LANES), BF16),
            pltpu.VMEM((heads, s, LANES), BF16),
            pltpu.VMEM((SCORE_RING, ATT_TQ, ATT_TQ), F32),
            pltpu.VMEM((heads * (s // ATT_TQ), ATT_TQ, LANES), F32),
        ],
        compiler_params=pltpu.CompilerParams(
            dimension_semantics=("arbitrary", "arbitrary"), vmem_limit_bytes=VMEM_LIMIT),
        name="fox_attn",
    )(pj, pj, pj, pj, aq, ak)


def _mix_kernel(g_ref, pooled_ref, zp_ref, h_ref, p_ref, wpool_ref, ps_ref, wout_ref,
                gpost_ref, wpg_ref, wpe_ref, out_ref):
    slabs_per_group = POOL_GROUP_DIM // LANES
    mixed = []
    for g in range(len(POOL_WINDOWS)):
        lhs = jnp.concatenate([pooled_ref[sl] for sl in range(g * slabs_per_group,
                                                              (g + 1) * slabs_per_group)], axis=1)
        mixed.append(jnp.dot(lhs, wpool_ref[g], preferred_element_type=F32))
    zp = jnp.concatenate([zp_ref[sl] for sl in range(SLABS_PER_COL)], axis=1).astype(F32)
    pool_out = (jnp.concatenate(mixed, axis=1) * ps_ref[...] * _silu(zp)).astype(BF16)

    attn_out = jnp.concatenate([g_ref[sl] for sl in range(PAIRS)], axis=1)
    mix = jnp.dot(attn_out, wout_ref[:ATTN_WIDTH, :], preferred_element_type=F32)
    mix = mix + jnp.dot(pool_out, wout_ref[ATTN_WIDTH:, :], preferred_element_type=F32)
    h1 = h_ref[...] + _rms_norm(mix, gpost_ref[...])
    gate = jax.nn.sigmoid(jnp.dot(h1.astype(BF16), wpg_ref[...], preferred_element_type=F32))
    pe = jnp.dot(p_ref[...].astype(BF16), wpe_ref[...], preferred_element_type=F32)
    out_ref[...] = h1 + gate * pe


def _mix_out(g, pj, h, p, w_pool, pool_scale, w_out, g_post, w_pg, w_pe, layer):
    b, s, _ = h.shape
    tm = MIX_TM

    def resident(arr):
        block = (None,) + arr.shape[1:]
        return pl.BlockSpec(block, lambda i, j: (layer,) + (0,) * (arr.ndim - 1),
                            pipeline_mode=pl.Buffered(1))

    def col_spec(col):
        return pl.BlockSpec((SLABS_PER_COL, None, tm, LANES), lambda i, j: (col, i, j, 0))

    return pl.pallas_call(
        _mix_kernel,
        grid=(b, s // tm),
        in_specs=[
            col_spec(0), col_spec(COL_U), col_spec(COL_ZP),
            pl.BlockSpec((None, tm, D_MODEL), lambda i, j: (i, j, 0)),
            pl.BlockSpec((None, None, tm, PLE_DIM), lambda i, j: (layer, i, j, 0)),
            resident(w_pool), resident(pool_scale), resident(w_out), resident(g_post),
            resident(w_pg), resident(w_pe),
        ],
        out_specs=pl.BlockSpec((None, tm, D_MODEL), lambda i, j: (i, j, 0)),
        out_shape=jax.ShapeDtypeStruct((b, s, D_MODEL), F32),
        compiler_params=pltpu.CompilerParams(
            dimension_semantics=("arbitrary", "arbitrary"), vmem_limit_bytes=VMEM_LIMIT),
        name="mix_out",
    )(g, pj, pj, h, p, w_pool, pool_scale, w_out, g_post, w_pg, w_pe)


def kernel(x, p, norm_pre, norm_post, w_in, b_f, w_pool, pool_scale, w_out, w_pg, w_pe):
    b, s, d = x.shape
    depth = w_in.shape[0]
    f_lo = 4 * ATTN_WIDTH
    f_hi = f_lo + HEADS
    lane_pad = LANES - HEADS
    w_a = w_in[:, :, :f_lo].astype(BF16)
    w_b = w_in[:, :, f_hi:].astype(BF16)
    w_f = jnp.pad(w_in[:, :, f_lo:f_hi], ((0, 0), (0, 0), (0, lane_pad))).astype(BF16)
    bf = jnp.pad(b_f, ((0, 0), (0, lane_pad))).reshape(depth, 1, LANES)
    g_pre = norm_pre.reshape(depth, 1, d)
    g_post = norm_post.reshape(depth, 1, d)
    ps = pool_scale.reshape(depth, 1, POOL_WIDTH)
    w_pool, w_out, w_pg, w_pe = (w.astype(BF16) for w in (w_pool, w_out, w_pg, w_pe))

    h = x
    for layer in range(depth):
        pj, aq, ak = _in_proj(h.reshape(b * s, d), g_pre, w_a, w_b, w_f, bf, layer, s)
        pj = pj.reshape(N_SLABS, b, s, LANES)
        g = _attention(pj, aq.reshape(b, s, LANES), ak.reshape(b, s, LANES))
        h = _mix_out(g, pj, h, p, w_pool, ps, w_out, g_post, w_pg, w_pe, layer)
    return h
```

```python
import functools
import math

import jax
import jax.numpy as jnp
import numpy as np
from jax import lax
from jax.experimental import pallas as pl
from jax.experimental.pallas import tpu as pltpu

F32 = jnp.float32
BF16 = jnp.bfloat16

D_MODEL = 1024
PLE_DIM = 256
HEADS = 16
HEAD_DIM = 64
ATTN_WIDTH = HEADS * HEAD_DIM
POOL_WINDOWS = (2, 4, 8, 16)
POOL_GROUP_DIM = 256
POOL_WIDTH = len(POOL_WINDOWS) * POOL_GROUP_DIM
EPS = 1e-6
LOG2E = math.log2(math.e)
Q_SCALE = LOG2E / math.sqrt(HEAD_DIM)

LANES = 128
HALO = 16
MAIN_COLS = 4 * ATTN_WIDTH + 2 * POOL_WIDTH
N_SLABS = MAIN_COLS // LANES
SLABS_PER_COL = ATTN_WIDTH // LANES
PAIRS = HEADS // 2
COL_Q, COL_K, COL_V, COL_ZA, COL_U, COL_ZP = range(6)
BIAS_COLS = 6
BIAS_STRIDE = 8

VMEM_LIMIT = 56 * 1024 * 1024

PROJ_TM = 512
PROJ_CHUNK = 256
ATT_TQ = 256
ATT_SLABS = 2
SCORE_LEAD = 6
SCORE_RING = 16
MIX_TM = 1024


def _rms_norm(x, g):
    return x * lax.rsqrt(jnp.mean(x * x, axis=-1, keepdims=True) + EPS) * g


def _silu(z):
    return z * jax.nn.sigmoid(z)


def _bias_lane(head):
    pair, odd = divmod(head, 2)
    return pair * BIAS_STRIDE + (0 if odd else HEAD_DIM)


def _bias_scatter_constants():
    scatter = np.zeros((3, LANES, 2 * LANES), np.float32)
    const = np.zeros((1, 2 * LANES), np.float32)
    for head in range(HEADS):
        lane = _bias_lane(head)
        for piece in range(3):
            scatter[piece, head, lane + piece] = 1.0
            scatter[piece, head, LANES + lane + 3 + piece] = -1.0
        const[0, lane + 3:lane + 6] = 1.0
        const[0, LANES + lane:LANES + lane + 3] = 1.0
    return jnp.asarray(scatter, BF16), jnp.asarray(const, F32)


def _in_proj_kernel(h_ref, g_ref, wa_ref, wb_ref, wf_ref, bf_ref, scat_ref, const_ref,
                    pj_ref, aq_ref, ak_ref, hn_ref, carry_ref, gate_ref, *, tiles_per_seq):
    tm = h_ref.shape[0]
    seq_tile = lax.rem(pl.program_id(0), tiles_per_seq)
    hn_ref[...] = _rms_norm(h_ref[...], g_ref[...]).astype(BF16)

    @pl.when(seq_tile == 0)
    def _():
        carry_ref[...] = jnp.zeros_like(carry_ref)
        gate_ref[...] = jnp.zeros_like(gate_ref)

    def gate_bias_stream():
        x = jnp.dot(hn_ref[...], wf_ref[...], preferred_element_type=F32) + bf_ref[...]
        c = jnp.minimum(x, 0.0) - jnp.log1p(jnp.exp(-jnp.abs(x)))
        yield
        row = lax.broadcasted_iota(jnp.int32, c.shape, 0)
        span = 1
        while span < tm:
            c = c + jnp.where(row >= span, pltpu.roll(c, span, 0), 0.0)
            span *= 2
            yield
        c = c + gate_ref[0:1, :]
        gate_ref[...] = jnp.broadcast_to(c[tm - 1:tm, :], gate_ref.shape)
        c = c * LOG2E
        hi = c.astype(BF16)
        mid = (c - hi.astype(F32)).astype(BF16)
        lo = (c - hi.astype(F32) - mid.astype(F32)).astype(BF16)
        yield
        bias = (jnp.dot(hi, scat_ref[0], preferred_element_type=F32)
                + jnp.dot(mid, scat_ref[1], preferred_element_type=F32)
                + jnp.dot(lo, scat_ref[2], preferred_element_type=F32)
                + const_ref[...])
        aq_ref[...] = bias[:, :LANES].astype(BF16)
        ak_ref[...] = bias[:, LANES:].astype(BF16)
        yield

    gate_steps = gate_bias_stream()
    pos =(seq_tile * tm + lax.broadcasted_iota(jnp.int32, (tm, 1), 0) + 1).astype(F32)
    slabs = PROJ_CHUNK // LANES
    n_chunks = MAIN_COLS // PROJ_CHUNK
    first_pool = COL_U * ATTN_WIDTH // PROJ_CHUNK
    pool_chunks = list(range(first_pool, first_pool + len(POOL_WINDOWS)))
    order = [c for c in range(n_chunks) if c not in pool_chunks]
    stride = len(order) // len(pool_chunks)
    for k, c in enumerate(pool_chunks):
        order.insert(k * (stride + 1) + 1, c)
    for c in order:
        w_ref, base = (wa_ref, 0) if c < first_pool else (wb_ref, first_pool)
        cols = slice((c - base) * PROJ_CHUNK, (c - base + 1) * PROJ_CHUNK)
        acc = jnp.dot(hn_ref[...], w_ref[:, cols], preferred_element_type=F32)
        if c * PROJ_CHUNK < ATTN_WIDTH:
            acc = acc * Q_SCALE
        if c in pool_chunks:
            window = POOL_WINDOWS[c - first_pool]
            held = slice((c - first_pool) * PROJ_CHUNK, (c - first_pool + 1) * PROJ_CHUNK)
            win = jnp.concatenate([carry_ref[:, held], acc], axis=0)
            carry_ref[:, held] = acc[tm - HALO:, :]
            span = 1
            while span < window:
                win = win + pltpu.roll(win, span, 0)
                span *= 2
            acc = win[HALO:] * (1.0 / jnp.minimum(pos, float(window))) - acc
        for s in range(slabs):
            pj_ref[c * slabs + s] = acc[:, s * LANES:(s + 1) * LANES].astype(BF16)
        next(gate_steps, None)
    for _ in gate_steps:
        pass


def _in_proj(h2d, g, w_a, w_b, w_f, bf, layer, seq):
    t = h2d.shape[0]
    tm = PROJ_TM
    assert seq % tm == 0 and PROJ_CHUNK == POOL_GROUP_DIM
    scatter, const = _bias_scatter_constants()
    bias_out = jax.ShapeDtypeStruct((t, LANES), BF16)
    return pl.pallas_call(
        functools.partial(_in_proj_kernel, tiles_per_seq=seq // tm),
        grid=(t // tm,),
        in_specs=[
            pl.BlockSpec((tm, D_MODEL), lambda i: (i, 0)),
            pl.BlockSpec((None, 1, D_MODEL), lambda i: (layer, 0, 0)),
            pl.BlockSpec((None, D_MODEL, 4 * ATTN_WIDTH), lambda i: (layer, 0, 0),
                         pipeline_mode=pl.Buffered(1)),
            pl.BlockSpec((None,) + w_b.shape[1:], lambda i: (layer, 0, 0),
                         pipeline_mode=pl.Buffered(1)),
            pl.BlockSpec((None, D_MODEL, LANES), lambda i: (layer, 0, 0),
                         pipeline_mode=pl.Buffered(1)),
            pl.BlockSpec((None, 1, LANES), lambda i: (layer, 0, 0)),
            pl.BlockSpec(scatter.shape, lambda i: (0, 0, 0)),
            pl.BlockSpec(const.shape, lambda i: (0, 0)),
        ],
        out_specs=[
            pl.BlockSpec((N_SLABS, tm, LANES), lambda i: (0, i, 0)),
            pl.BlockSpec((tm, LANES), lambda i: (i, 0)),
            pl.BlockSpec((tm, LANES), lambda i: (i, 0)),
        ],
        out_shape=[jax.ShapeDtypeStruct((N_SLABS, t, LANES), BF16), bias_out, bias_out],
        scratch_shapes=[pltpu.VMEM((tm, D_MODEL), BF16),
                        pltpu.VMEM((HALO, POOL_WIDTH), F32),
                        pltpu.VMEM((8, LANES), F32)],
        compiler_params=pltpu.CompilerParams(
            dimension_semantics=("arbitrary",), vmem_limit_bytes=VMEM_LIMIT),
        name="in_proj",
    )(h2d, g, w_a, w_b, w_f, bf, scatter, const)


_NT = (((1,), (1,)), ((), ()))


def _fold(x, op):
    out = x[:, :LANES]
    for g in range(1, x.shape[1] // LANES):
        out = op(out, x[:, g * LANES:(g + 1) * LANES])
    return out


def _attn_kernel(q_ref, k_ref, v_ref, z_ref, aq_ref, ak_ref, o_ref,
                 qa_ref, ka_ref, va_ref, s_ref, m_ref):
    n_slabs, seq, _ = q_ref.shape
    n_heads = 2 * n_slabs
    tq = ATT_TQ
    lane = lax.broadcasted_iota(jnp.int32, (1, LANES), 1)

    for hh in range(n_heads):
        sl, h = divmod(hh, 2)
        pair = pl.program_id(1) * n_slabs + sl
        first = pair * BIAS_STRIDE + (HEAD_DIM if h == 0 else 0)
        data = (lane < HEAD_DIM) if h == 0 else (lane >= HEAD_DIM)
        dmask = jnp.where(data, 1.0, 0.0).astype(BF16)
        bmask = jnp.where((lane >= first) & (lane < first + BIAS_COLS), 1.0, 0.0).astype(BF16)
        qa_ref[hh] = q_ref[sl] * dmask + aq_ref[0] * bmask
        ka_ref[hh] = k_ref[sl] * dmask + ak_ref[0] * bmask
        va_ref[hh] = v_ref[sl] * dmask + (1.0 - dmask)

    tri = (lax.broadcasted_iota(jnp.int32, (tq, tq), 1)
           <= lax.broadcasted_iota(jnp.int32, (tq, tq), 0))
    lane_t = lax.broadcasted_iota(jnp.int32, (tq, LANES), 1)

    units = [(qi, hh) for qi in reversed(range(seq // tq)) for hh in range(n_heads)]
    unit_end = np.cumsum([qi + 1 for qi, _ in units])
    n_items = int(unit_end[-1])
    ring = s_ref.shape[0]

    def tile(idx):
        return slice(idx * tq, (idx + 1) * tq)

    def score_stream():
        item = 0
        for u, (qi, h) in enumerate(units):
            qt = qa_ref[h, tile(qi), :]
            m_run = None
            for kc in range(qi + 1):
                s = lax.dot_general(qt, ka_ref[h, tile(kc), :], _NT, preferred_element_type=F32)
                if kc == qi:
                    s = jnp.where(tri, s, -jnp.inf)
                s_ref[item % ring] = s
                fm = _fold(s, jnp.maximum)
                m_run = fm if m_run is None else jnp.maximum(m_run, fm)
                item += 1
                if kc == qi:
                    m_ref[u] = jnp.broadcast_to(jnp.max(m_run, axis=1, keepdims=True),
                                                (tq, LANES))
                yield

    def value_stream():
        item = 0
        outs = {}
        for u, (qi, h) in enumerate(units):
            m = m_ref[u]
            acc = None
            for kc in range(qi + 1):
                s = s_ref[item % ring]
                p = jnp.concatenate(
                    [jnp.exp2(s[:, g * LANES:(g + 1) * LANES] - m) for g in range(tq // LANES)],
                    axis=1).astype(BF16)
                pv = jnp.dot(p, va_ref[h, tile(kc), :], preferred_element_type=F32)
                acc = pv if acc is None else acc + pv
                item += 1
                if kc == qi:
                    outs[h] = acc / pltpu.roll(acc, HEAD_DIM, 1)
                    if h % 2 == 1:
                        o = jnp.where(lane_t < HEAD_DIM, outs[h - 1], outs[h])
                        z = z_ref[h // 2, tile(qi), :].astype(F32)
                        o_ref[h // 2, tile(qi), :] = (o * _silu(z)).astype(BF16)
                yield

    scores, values = score_stream(), value_stream()
    scored = 0
    for item in range(n_items):
        unit = int(np.searchsorted(unit_end, item, side="right"))
        target = min(n_items, int(unit_end[unit]) + SCORE_LEAD)
        assert target - item <= ring
        while scored < target:
            next(scores)
            scored += 1
        next(values)


def _attention(pj, aq, ak):
    _, b, s, _ = pj.shape
    n = ATT_SLABS
    heads = 2 * n

    def slab_spec(col):
        return pl.BlockSpec((n, None, s, LANES),
                            lambda i, j: (col * (SLABS_PER_COL // n) + j, i, 0, 0))

    bias_spec = pl.BlockSpec((1, s, LANES), lambda i, j: (i, 0, 0))
    return pl.pallas_call(
        _attn_kernel,
        grid=(b, PAIRS // n),
        in_specs=[slab_spec(COL_Q), slab_spec(COL_K), slab_spec(COL_V), slab_spec(COL_ZA),
                  bias_spec, bias_spec],
        out_specs=pl.BlockSpec((n, None, s, LANES), lambda i, j: (j, i, 0, 0)),
        out_shape=jax.ShapeDtypeStruct((PAIRS, b, s, LANES), BF16),
        scratch_shapes=[
            pltpu.VMEM((heads, s, LANES), BF16),
            pltpu.VMEM((heads, s, LANES), BF16),
            pltpu.VMEM((heads, s, LANES), BF16),
            pltpu.VMEM((SCORE_RING, ATT_TQ, ATT_TQ), F32),
            pltpu.VMEM((heads * (s // ATT_TQ), ATT_TQ, LANES), F32),
        ],
        compiler_params=pltpu.CompilerParams(
            dimension_semantics=("arbitrary", "arbitrary"), vmem_limit_bytes=VMEM_LIMIT),
        name="fox_attn",
    )(pj, pj, pj, pj, aq, ak)


def _mix_kernel(g_ref, pooled_ref, zp_ref, h_ref, p_ref, wpool_ref, ps_ref, wout_ref,
                gpost_ref, wpg_ref, wpe_ref, out_ref):
    slabs_per_group = POOL_GROUP_DIM // LANES
    mixed = []
    for g in range(len(POOL_WINDOWS)):
        lhs = jnp.concatenate([pooled_ref[sl] for sl in range(g * slabs_per_group,
                                                              (g + 1) * slabs_per_group)], axis=1)
        mixed.append(jnp.dot(lhs, wpool_ref[g], preferred_element_type=F32))
    zp = jnp.concatenate([zp_ref[sl] for sl in range(SLABS_PER_COL)], axis=1).astype(F32)
    pool_out = (jnp.concatenate(mixed, axis=1) * ps_ref[...] * _silu(zp)).astype(BF16)

    attn_out = jnp.concatenate([g_ref[sl] for sl in range(PAIRS)], axis=1)
    mix = jnp.dot(attn_out, wout_ref[:ATTN_WIDTH, :], preferred_element_type=F32)
    mix = mix + jnp.dot(pool_out, wout_ref[ATTN_WIDTH:, :], preferred_element_type=F32)
    h1 = h_ref[...] + _rms_norm(mix, gpost_ref[...])
    gate = jax.nn.sigmoid(jnp.dot(h1.astype(BF16), wpg_ref[...], preferred_element_type=F32))
    pe = jnp.dot(p_ref[...].astype(BF16), wpe_ref[...], preferred_element_type=F32)
    out_ref[...] = h1 + gate * pe


def _mix_out(g, pj, h, p, w_pool, pool_scale, w_out, g_post, w_pg, w_pe, layer):
    b, s, _ = h.shape
    tm = MIX_TM

    def resident(arr):
        block = (None,) + arr.shape[1:]
        return pl.BlockSpec(block, lambda i, j: (layer,) + (0,) * (arr.ndim - 1),
                            pipeline_mode=pl.Buffered(1))

    def col_spec(col):
        return pl.BlockSpec((SLABS_PER_COL, None, tm, LANES), lambda i, j: (col, i, j, 0))

    return pl.pallas_call(
        _mix_kernel,
        grid=(b, s // tm),
        in_specs=[
            col_spec(0), col_spec(COL_U), col_spec(COL_ZP),
            pl.BlockSpec((None, tm, D_MODEL), lambda i, j: (i, j, 0)),
            pl.BlockSpec((None, None, tm, PLE_DIM), lambda i, j: (layer, i, j, 0)),
            resident(w_pool), resident(pool_scale), resident(w_out), resident(g_post),
            resident(w_pg), resident(w_pe),
        ],
        out_specs=pl.BlockSpec((None, tm, D_MODEL), lambda i, j: (i, j, 0)),
        out_shape=jax.ShapeDtypeStruct((b, s, D_MODEL), F32),
        compiler_params=pltpu.CompilerParams(
            dimension_semantics=("arbitrary", "arbitrary"), vmem_limit_bytes=VMEM_LIMIT),
        name="mix_out",
    )(g, pj, pj, h, p, w_pool, pool_scale, w_out, g_post, w_pg, w_pe)


def kernel(x, p, norm_pre, norm_post, w_in, b_f, w_pool, pool_scale, w_out, w_pg, w_pe):
    b, s, d = x.shape
    depth = w_in.shape[0]
    f_lo = 4 * ATTN_WIDTH
    f_hi = f_lo + HEADS
    lane_pad = LANES - HEADS
    w_in = w_in.astype(BF16)
    w_a = w_in
    w_b = w_in[:, :, f_hi:]
    w_f = jnp.pad(w_in[:, :, f_lo:f_hi], ((0, 0), (0, 0), (0, lane_pad)))
    bf = jnp.pad(b_f, ((0, 0), (0, lane_pad))).reshape(depth, 1, LANES)
    g_pre = norm_pre.reshape(depth, 1, d)
    g_post = norm_post.reshape(depth, 1, d)
    ps = pool_scale.reshape(depth, 1, POOL_WIDTH)
    w_pool, w_out, w_pg, w_pe = (w.astype(BF16) for w in (w_pool, w_out, w_pg, w_pe))

    h = x
    for layer in range(depth):
        pj, aq, ak = _in_proj(h.reshape(b * s, d), g_pre, w_a, w_b, w_f, bf, layer, s)
        pj = pj.reshape(N_SLABS, b, s, LANES)
        g = _attention(pj, aq.reshape(b, s, LANES), ak.reshape(b, s, LANES))
        h = _mix_out(g, pj, h, p, w_pool, ps, w_out, g_post, w_pg, w_pe, layer)
    return h
```

```python
import functools
import math

import jax
import jax.numpy as jnp
import numpy as np
from jax import lax
from jax.experimental import pallas as pl
from jax.experimental.pallas import tpu as pltpu

F32 = jnp.float32
BF16 = jnp.bfloat16

D_MODEL = 1024
PLE_DIM = 256
HEADS = 16
HEAD_DIM = 64
ATTN_WIDTH = HEADS * HEAD_DIM
POOL_WINDOWS = (2, 4, 8, 16)
POOL_GROUP_DIM = 256
POOL_WIDTH = len(POOL_WINDOWS) * POOL_GROUP_DIM
EPS = 1e-6
LOG2E = math.log2(math.e)
Q_SCALE = LOG2E / math.sqrt(HEAD_DIM)

LANES = 128
HALO = 16
MAIN_COLS = 4 * ATTN_WIDTH + 2 * POOL_WIDTH
N_SLABS = MAIN_COLS // LANES
SLABS_PER_COL = ATTN_WIDTH // LANES
PAIRS = HEADS // 2
COL_Q, COL_K, COL_V, COL_ZA, COL_U, COL_ZP = range(6)
BIAS_COLS = 6
BIAS_STRIDE = 8

VMEM_LIMIT = 56 * 1024 * 1024

PROJ_TM = 1024
PROJ_CHUNK = 256
ATT_TQ = 256
ATT_SLABS = 2
SCORE_LEAD = 6
SCORE_RING = 16
MIX_TM = 1024


def _rms_norm(x, g):
    return x * lax.rsqrt(jnp.mean(x * x, axis=-1, keepdims=True) + EPS) * g


def _silu(z):
    return z * jax.nn.sigmoid(z)


def _bias_lane(head):
    pair, odd = divmod(head, 2)
    return pair * BIAS_STRIDE + (0 if odd else HEAD_DIM)


def _bias_scatter_constants():
    scatter = np.zeros((3, LANES, 2 * LANES), np.float32)
    const = np.zeros((1, 2 * LANES), np.float32)
    for head in range(HEADS):
        lane = _bias_lane(head)
        for piece in range(3):
            scatter[piece, head, lane + piece] = 1.0
            scatter[piece, head, LANES + lane + 3 + piece] = -1.0
        const[0, lane + 3:lane + 6] = 1.0
        const[0, LANES + lane:LANES + lane + 3] = 1.0
    return jnp.asarray(scatter, BF16), jnp.asarray(const, F32)


def _in_proj_kernel(h_ref, g_ref, wa_ref, wb_ref, wf_ref, bf_ref, scat_ref, const_ref,
                    pj_ref, aq_ref, ak_ref, hn_ref, carry_ref, gate_ref, *, tiles_per_seq):
    tm = h_ref.shape[0]
    seq_tile = lax.rem(pl.program_id(0), tiles_per_seq)
    hn_ref[...] = _rms_norm(h_ref[...], g_ref[...]).astype(BF16)

    @pl.when(seq_tile == 0)
    def _():
        carry_ref[...] = jnp.zeros_like(carry_ref)
        gate_ref[...] = jnp.zeros_like(gate_ref)

    def gate_bias_stream():
        x = jnp.dot(hn_ref[...], wf_ref[...], preferred_element_type=F32) + bf_ref[...]
        c = jnp.minimum(x, 0.0) - jnp.log1p(jnp.exp(-jnp.abs(x)))
        yield
        row = lax.broadcasted_iota(jnp.int32, c.shape, 0)
        span = 1
        while span < tm:
            c = c + jnp.where(row >= span, pltpu.roll(c, span, 0), 0.0)
            span *= 2
            yield
        c = c + gate_ref[0:1, :]
        gate_ref[...] = jnp.broadcast_to(c[tm - 1:tm, :], gate_ref.shape)
        c = c * LOG2E
        hi = c.astype(BF16)
        mid = (c - hi.astype(F32)).astype(BF16)
        lo = (c - hi.astype(F32) - mid.astype(F32)).astype(BF16)
        yield
        bias = (jnp.dot(hi, scat_ref[0], preferred_element_type=F32)
                + jnp.dot(mid, scat_ref[1], preferred_element_type=F32)
                + jnp.dot(lo, scat_ref[2], preferred_element_type=F32)
                + const_ref[...])
        aq_ref[...] = bias[:, :LANES].astype(BF16)
        ak_ref[...] = bias[:, LANES:].astype(BF16)
        yield

    gate_steps = gate_bias_stream()
    pos =(seq_tile * tm + lax.broadcasted_iota(jnp.int32, (tm, 1), 0) + 1).astype(F32)
    slabs = PROJ_CHUNK // LANES
    n_chunks = MAIN_COLS // PROJ_CHUNK
    first_pool = COL_U * ATTN_WIDTH // PROJ_CHUNK
    pool_chunks = list(range(first_pool, first_pool + len(POOL_WINDOWS)))
    order = [c for c in range(n_chunks) if c not in pool_chunks]
    stride = len(order) // len(pool_chunks)
    for k, c in enumerate(pool_chunks):
        order.insert(k * (stride + 1) + 1, c)
    for c in order:
        w_ref, base = (wa_ref, 0) if c < first_pool else (wb_ref, first_pool)
        cols = slice((c - base) * PROJ_CHUNK, (c - base + 1) * PROJ_CHUNK)
        acc = jnp.dot(hn_ref[...], w_ref[:, cols], preferred_element_type=F32)
        if c * PROJ_CHUNK < ATTN_WIDTH:
            acc = acc * Q_SCALE
        if c in pool_chunks:
            window = POOL_WINDOWS[c - first_pool]
            held = slice((c - first_pool) * PROJ_CHUNK, (c - first_pool + 1) * PROJ_CHUNK)
            win = jnp.concatenate([carry_ref[:, held], acc], axis=0)
            carry_ref[:, held] = acc[tm - HALO:, :]
            span = 1
            while span < window:
                win = win + pltpu.roll(win, span, 0)
                span *= 2
            acc = win[HALO:] * (1.0 / jnp.minimum(pos, float(window))) - acc
        for s in range(slabs):
            pj_ref[c * slabs + s] = acc[:, s * LANES:(s + 1) * LANES].astype(BF16)
        next(gate_steps, None)
    for _ in gate_steps:
        pass


def _in_proj(h2d, g, w_a, w_b, w_f, bf, layer, seq):
    t = h2d.shape[0]
    tm = PROJ_TM
    assert seq % tm == 0 and PROJ_CHUNK == POOL_GROUP_DIM
    scatter, const = _bias_scatter_constants()
    bias_out = jax.ShapeDtypeStruct((t, LANES), BF16)
    return pl.pallas_call(
        functools.partial(_in_proj_kernel, tiles_per_seq=seq // tm),
        grid=(t // tm,),
        in_specs=[
            pl.BlockSpec((tm, D_MODEL), lambda i: (i, 0)),
            pl.BlockSpec((None, 1, D_MODEL), lambda i: (layer, 0, 0)),
            pl.BlockSpec((None, D_MODEL, 4 * ATTN_WIDTH), lambda i: (layer, 0, 0),
                         pipeline_mode=pl.Buffered(1)),
            pl.BlockSpec((None,) + w_b.shape[1:], lambda i: (layer, 0, 0),
                         pipeline_mode=pl.Buffered(1)),
            pl.BlockSpec((None, D_MODEL, LANES), lambda i: (layer, 0, 0),
                         pipeline_mode=pl.Buffered(1)),
            pl.BlockSpec((None, 1, LANES), lambda i: (layer, 0, 0)),
            pl.BlockSpec(scatter.shape, lambda i: (0, 0, 0)),
            pl.BlockSpec(const.shape, lambda i: (0, 0)),
        ],
        out_specs=[
            pl.BlockSpec((N_SLABS, tm, LANES), lambda i: (0, i, 0)),
            pl.BlockSpec((tm, LANES), lambda i: (i, 0)),
            pl.BlockSpec((tm, LANES), lambda i: (i, 0)),
        ],
        out_shape=[jax.ShapeDtypeStruct((N_SLABS, t, LANES), BF16), bias_out, bias_out],
        scratch_shapes=[pltpu.VMEM((tm, D_MODEL), BF16),
                        pltpu.VMEM((HALO, POOL_WIDTH), F32),
                        pltpu.VMEM((8, LANES), F32)],
        compiler_params=pltpu.CompilerParams(
            dimension_semantics=("arbitrary",), vmem_limit_bytes=VMEM_LIMIT),
        name="in_proj",
    )(h2d, g, w_a, w_b, w_f, bf, scatter, const)


_NT = (((1,), (1,)), ((), ()))


def _fold(x, op):
    out = x[:, :LANES]
    for g in range(1, x.shape[1] // LANES):
        out = op(out, x[:, g * LANES:(g + 1) * LANES])
    return out


def _attn_kernel(q_ref, k_ref, v_ref, z_ref, aq_ref, ak_ref, o_ref,
                 qa_ref, ka_ref, va_ref, s_ref, m_ref):
    n_slabs, seq, _ = q_ref.shape
    n_heads = 2 * n_slabs
    tq = ATT_TQ
    lane = lax.broadcasted_iota(jnp.int32, (1, LANES), 1)

    for hh in range(n_heads):
        sl, h = divmod(hh, 2)
        pair = pl.program_id(1) * n_slabs + sl
        first = pair * BIAS_STRIDE + (HEAD_DIM if h == 0 else 0)
        data = (lane < HEAD_DIM) if h == 0 else (lane >= HEAD_DIM)
        dmask = jnp.where(data, 1.0, 0.0).astype(BF16)
        bmask = jnp.where((lane >= first) & (lane < first + BIAS_COLS), 1.0, 0.0).astype(BF16)
        qa_ref[hh] = q_ref[sl] * dmask + aq_ref[0] * bmask
        ka_ref[hh] = k_ref[sl] * dmask + ak_ref[0] * bmask
        va_ref[hh] = v_ref[sl] * dmask + (1.0 - dmask)

    tri = (lax.broadcasted_iota(jnp.int32, (tq, tq), 1)
           <= lax.broadcasted_iota(jnp.int32, (tq, tq), 0))
    lane_t = lax.broadcasted_iota(jnp.int32, (tq, LANES), 1)

    units = [(qi, hh) for qi in reversed(range(seq // tq)) for hh in range(n_heads)]
    unit_end = np.cumsum([qi + 1 for qi, _ in units])
    n_items = int(unit_end[-1])
    ring = s_ref.shape[0]

    def tile(idx):
        return slice(idx * tq, (idx + 1) * tq)

    def score_stream():
        item = 0
        for u, (qi, h) in enumerate(units):
            qt = qa_ref[h, tile(qi), :]
            m_run = None
            for kc in range(qi + 1):
                s = lax.dot_general(qt, ka_ref[h, tile(kc), :], _NT, preferred_element_type=F32)
                if kc == qi:
                    s = jnp.where(tri, s, -jnp.inf)
                s_ref[item % ring] = s
                fm = _fold(s, jnp.maximum)
                m_run = fm if m_run is None else jnp.maximum(m_run, fm)
                item += 1
                if kc == qi:
                    m_ref[u] = jnp.broadcast_to(jnp.max(m_run, axis=1, keepdims=True),
                                                (tq, LANES))
                yield

    def value_stream():
        item = 0
        outs = {}
        for u, (qi, h) in enumerate(units):
            m = m_ref[u]
            acc = None
            for kc in range(qi + 1):
                s = s_ref[item % ring]
                p = jnp.concatenate(
                    [jnp.exp2(s[:, g * LANES:(g + 1) * LANES] - m) for g in range(tq // LANES)],
                    axis=1).astype(BF16)
                pv = jnp.dot(p, va_ref[h, tile(kc), :], preferred_element_type=F32)
                acc = pv if acc is None else acc + pv
                item += 1
                if kc == qi:
                    outs[h] = acc / pltpu.roll(acc, HEAD_DIM, 1)
                    if h % 2 == 1:
                        o = jnp.where(lane_t < HEAD_DIM, outs[h - 1], outs[h])
                        z = z_ref[h // 2, tile(qi), :].astype(F32)
                        o_ref[h // 2, tile(qi), :] = (o * _silu(z)).astype(BF16)
                yield

    scores, values = score_stream(), value_stream()
    scored = 0
    for item in range(n_items):
        unit = int(np.searchsorted(unit_end, item, side="right"))
        target = min(n_items, int(unit_end[unit]) + SCORE_LEAD)
        assert target - item <= ring
        while scored < target:
            next(scores)
            scored += 1
        next(values)


def _attention(pj, aq, ak):
    _, b, s, _ = pj.shape
    n = ATT_SLABS
    heads = 2 * n

    def slab_spec(col):
        return pl.BlockSpec((n, None, s, LANES),
                            lambda i, j: (col * (SLABS_PER_COL // n) + j, i, 0, 0))

    bias_spec = pl.BlockSpec((1, s, LANES), lambda i, j: (i, 0, 0))
    return pl.pallas_call(
        _attn_kernel,
        grid=(b, PAIRS // n),
        in_specs=[slab_spec(COL_Q), slab_spec(COL_K), slab_spec(COL_V), slab_spec(COL_ZA),
                  bias_spec, bias_spec],
        out_specs=pl.BlockSpec((n, None, s, LANES), lambda i, j: (j, i, 0, 0)),
        out_shape=jax.ShapeDtypeStruct((PAIRS, b, s, LANES), BF16),
        scratch_shapes=[
            pltpu.VMEM((heads, s, LANES), BF16),
            pltpu.VMEM((heads, s, LANES), BF16),
            pltpu.VMEM((heads, s, LANES), BF16),
            pltpu.VMEM((SCORE_RING, ATT_TQ, ATT_TQ), F32),
            pltpu.VMEM((heads * (s // ATT_TQ), ATT_TQ, LANES), F32),
        ],
        compiler_params=pltpu.CompilerParams(
            dimension_semantics=("arbitrary", "arbitrary"), vmem_limit_bytes=VMEM_LIMIT),
        name="fox_attn",
    )(pj, pj, pj, pj, aq, ak)


def _mix_kernel(g_ref, pooled_ref, zp_ref, h_ref, p_ref, wpool_ref, ps_ref, wout_ref,
                gpost_ref, wpg_ref, wpe_ref, out_ref):
    slabs_per_group = POOL_GROUP_DIM // LANES
    mixed = []
    for g in range(len(POOL_WINDOWS)):
        lhs = jnp.concatenate([pooled_ref[sl] for sl in range(g * slabs_per_group,
                                                              (g + 1) * slabs_per_group)], axis=1)
        mixed.append(jnp.dot(lhs, wpool_ref[g], preferred_element_type=F32))
    zp = jnp.concatenate([zp_ref[sl] for sl in range(SLABS_PER_COL)], axis=1).astype(F32)
    pool_out = (jnp.concatenate(mixed, axis=1) * ps_ref[...] * _silu(zp)).astype(BF16)

    attn_out = jnp.concatenate([g_ref[sl] for sl in range(PAIRS)], axis=1)
    mix = jnp.dot(attn_out, wout_ref[:ATTN_WIDTH, :], preferred_element_type=F32)
    mix = mix + jnp.dot(pool_out, wout_ref[ATTN_WIDTH:, :], preferred_element_type=F32)
    h1 = h_ref[...] + _rms_norm(mix, gpost_ref[...])
    gate = jax.nn.sigmoid(jnp.dot(h1.astype(BF16), wpg_ref[...], preferred_element_type=F32))
    pe = jnp.dot(p_ref[...].astype(BF16), wpe_ref[...], preferred_element_type=F32)
    out_ref[...] = h1 + gate * pe


def _mix_out(g, pj, h, p, w_pool, pool_scale, w_out, g_post, w_pg, w_pe, layer):
    b, s, _ = h.shape
    tm = MIX_TM

    def resident(arr):
        block = (None,) + arr.shape[1:]
        return pl.BlockSpec(block, lambda i, j: (layer,) + (0,) * (arr.ndim - 1),
                            pipeline_mode=pl.Buffered(1))

    def col_spec(col):
        return pl.BlockSpec((SLABS_PER_COL, None, tm, LANES), lambda i, j: (col, i, j, 0))

    return pl.pallas_call(
        _mix_kernel,
        grid=(b, s // tm),
        in_specs=[
            col_spec(0), col_spec(COL_U), col_spec(COL_ZP),
            pl.BlockSpec((None, tm, D_MODEL), lambda i, j: (i, j, 0)),
            pl.BlockSpec((None, None, tm, PLE_DIM), lambda i, j: (layer, i, j, 0)),
            resident(w_pool), resident(pool_scale), resident(w_out), resident(g_post),
            resident(w_pg), resident(w_pe),
        ],
        out_specs=pl.BlockSpec((None, tm, D_MODEL), lambda i, j: (i, j, 0)),
        out_shape=jax.ShapeDtypeStruct((b, s, D_MODEL), F32),
        compiler_params=pltpu.CompilerParams(
            dimension_semantics=("arbitrary", "arbitrary"), vmem_limit_bytes=VMEM_LIMIT),
        name="mix_out",
    )(g, pj, pj, h, p, w_pool, pool_scale, w_out, g_post, w_pg, w_pe)


def kernel(x, p, norm_pre, norm_post, w_in, b_f, w_pool, pool_scale, w_out, w_pg, w_pe):
    b, s, d = x.shape
    depth = w_in.shape[0]
    f_lo = 4 * ATTN_WIDTH
    f_hi = f_lo + HEADS
    lane_pad = LANES - HEADS
    w_in = w_in.astype(BF16)
    w_a = w_in
    w_b = w_in[:, :, f_hi:]
    w_f = jnp.pad(w_in[:, :, f_lo:f_hi], ((0, 0), (0, 0), (0, lane_pad)))
    bf = jnp.pad(b_f, ((0, 0), (0, lane_pad))).reshape(depth, 1, LANES)
    g_pre = norm_pre.reshape(depth, 1, d)
    g_post = norm_post.reshape(depth, 1, d)
    ps = pool_scale.reshape(depth, 1, POOL_WIDTH)
    w_pool, w_out, w_pg, w_pe = (w.astype(BF16) for w in (w_pool, w_out, w_pg, w_pe))

    h = x
    for layer in range(depth):
        pj, aq, ak = _in_proj(h.reshape(b * s, d), g_pre, w_a, w_b, w_f, bf, layer, s)
        pj = pj.reshape(N_SLABS, b, s, LANES)
        g = _attention(pj, aq.reshape(b, s, LANES), ak.reshape(b, s, LANES))
        h = _mix_out(g, pj, h, p, w_pool, ps, w_out, g_post, w_pg, w_pe, layer)
    return h
```

```python
import functools
import math

import jax
import jax.numpy as jnp
import numpy as np
from jax import lax
from jax.experimental import pallas as pl
from jax.experimental.pallas import tpu as pltpu

F32 = jnp.float32
BF16 = jnp.bfloat16

D_MODEL = 1024
PLE_DIM = 256
HEADS = 16
HEAD_DIM = 64
ATTN_WIDTH = HEADS * HEAD_DIM
POOL_WINDOWS = (2, 4, 8, 16)
POOL_GROUP_DIM = 256
POOL_WIDTH = len(POOL_WINDOWS) * POOL_GROUP_DIM
EPS = 1e-6
LOG2E = math.log2(math.e)
Q_SCALE = LOG2E / math.sqrt(HEAD_DIM)

LANES = 128
HALO = 16
MAIN_COLS = 4 * ATTN_WIDTH + 2 * POOL_WIDTH
N_SLABS = MAIN_COLS // LANES
SLABS_PER_COL = ATTN_WIDTH // LANES
PAIRS = HEADS // 2
COL_Q, COL_K, COL_V, COL_ZA, COL_U, COL_ZP = range(6)
BIAS_COLS = 6
BIAS_STRIDE = 8

VMEM_LIMIT = 56 * 1024 * 1024

PROJ_TM = 1024
PROJ_CHUNK = 256
ATT_TQ = 256
ATT_SLABS = 2
SCORE_LEAD = 6
SCORE_RING = 16
MIX_TM = 1024


def _rms_norm(x, g):
    return x * lax.rsqrt(jnp.mean(x * x, axis=-1, keepdims=True) + EPS) * g


def _silu(z):
    return z * jax.nn.sigmoid(z)


def _bias_lane(head):
    pair, odd = divmod(head, 2)
    return pair * BIAS_STRIDE + (0 if odd else HEAD_DIM)


def _bias_scatter_constants():
    scatter = np.zeros((3, LANES, 2 * LANES), np.float32)
    const = np.zeros((1, 2 * LANES), np.float32)
    for head in range(HEADS):
        lane = _bias_lane(head)
        for piece in range(3):
            scatter[piece, head, lane + piece] = 1.0
            scatter[piece, head, LANES + lane + 3 + piece] = -1.0
        const[0, lane + 3:lane + 6] = 1.0
        const[0, LANES + lane:LANES + lane + 3] = 1.0
    return jnp.asarray(scatter, BF16), jnp.asarray(const, F32)


def _in_proj_kernel(h_ref, g_ref, wa_ref, wb_ref, wf_ref, bf_ref, scat_ref, const_ref,
                    pj_ref, aq_ref, ak_ref, hn_ref, carry_ref, gate_ref, *, tiles_per_seq):
    tm = h_ref.shape[0]
    seq_tile = lax.rem(pl.program_id(0), tiles_per_seq)
    hn_ref[...] = _rms_norm(h_ref[...], g_ref[...]).astype(BF16)

    @pl.when(seq_tile == 0)
    def _():
        carry_ref[...] = jnp.zeros_like(carry_ref)
        gate_ref[...] = jnp.zeros_like(gate_ref)

    def gate_bias_stream():
        x = jnp.dot(hn_ref[...], wf_ref[...], preferred_element_type=F32) + bf_ref[...]
        c = jnp.minimum(x, 0.0) - jnp.log1p(jnp.exp(-jnp.abs(x)))
        yield
        row = lax.broadcasted_iota(jnp.int32, c.shape, 0)
        span = 1
        while span < tm:
            c = c + jnp.where(row >= span, pltpu.roll(c, span, 0), 0.0)
            span *= 2
            yield
        c = c + gate_ref[0:1, :]
        gate_ref[...] = jnp.broadcast_to(c[tm - 1:tm, :], gate_ref.shape)
        c = c * LOG2E
        hi = c.astype(BF16)
        mid = (c - hi.astype(F32)).astype(BF16)
        lo = (c - hi.astype(F32) - mid.astype(F32)).astype(BF16)
        yield
        bias = (jnp.dot(hi, scat_ref[0], preferred_element_type=F32)
                + jnp.dot(mid, scat_ref[1], preferred_element_type=F32)
                + jnp.dot(lo, scat_ref[2], preferred_element_type=F32)
                + const_ref[...])
        aq_ref[...] = bias[:, :LANES].astype(BF16)
        ak_ref[...] = bias[:, LANES:].astype(BF16)
        yield

    gate_steps = gate_bias_stream()
    pos =(seq_tile * tm + lax.broadcasted_iota(jnp.int32, (tm, 1), 0) + 1).astype(F32)
    slabs = PROJ_CHUNK // LANES
    n_chunks = MAIN_COLS // PROJ_CHUNK
    first_pool = COL_U * ATTN_WIDTH // PROJ_CHUNK
    pool_chunks = list(range(first_pool, first_pool + len(POOL_WINDOWS)))
    order = [c for c in range(n_chunks) if c not in pool_chunks]
    stride = len(order) // len(pool_chunks)
    for k, c in enumerate(pool_chunks):
        order.insert(k * (stride + 1) + 1, c)
    for c in order:
        w_ref, base = (wa_ref, 0) if c < first_pool else (wb_ref, first_pool)
        cols = slice((c - base) * PROJ_CHUNK, (c - base + 1) * PROJ_CHUNK)
        acc = jnp.dot(hn_ref[...], w_ref[:, cols], preferred_element_type=F32)
        if c * PROJ_CHUNK < ATTN_WIDTH:
            acc = acc * Q_SCALE
        if c in pool_chunks:
            window = POOL_WINDOWS[c - first_pool]
            held = slice((c - first_pool) * PROJ_CHUNK, (c - first_pool + 1) * PROJ_CHUNK)
            win = jnp.concatenate([carry_ref[:, held], acc], axis=0)
            carry_ref[:, held] = acc[tm - HALO:, :]
            span = 1
            while span < window:
                win = win + pltpu.roll(win, span, 0)
                span *= 2
            acc = win[HALO:] * (1.0 / jnp.minimum(pos, float(window))) - acc
        for s in range(slabs):
            pj_ref[c * slabs + s] = acc[:, s * LANES:(s + 1) * LANES].astype(BF16)
        next(gate_steps, None)
    for _ in gate_steps:
        pass


def _in_proj(h2d, g, w_a, w_b, w_f, bf, layer, seq):
    t = h2d.shape[0]
    tm = PROJ_TM
    assert seq % tm == 0 and PROJ_CHUNK == POOL_GROUP_DIM
    scatter, const = _bias_scatter_constants()
    bias_out = jax.ShapeDtypeStruct((t, LANES), BF16)
    return pl.pallas_call(
        functools.partial(_in_proj_kernel, tiles_per_seq=seq // tm),
        grid=(t // tm,),
        in_specs=[
            pl.BlockSpec((tm, D_MODEL), lambda i: (i, 0)),
            pl.BlockSpec((None, 1, D_MODEL), lambda i: (layer, 0, 0)),
            pl.BlockSpec((None, D_MODEL, 4 * ATTN_WIDTH), lambda i: (layer, 0, 0),
                         pipeline_mode=pl.Buffered(1)),
            pl.BlockSpec((None,) + w_b.shape[1:], lambda i: (layer, 0, 0),
                         pipeline_mode=pl.Buffered(1)),
            pl.BlockSpec((None, D_MODEL, LANES), lambda i: (layer, 0, 0),
                         pipeline_mode=pl.Buffered(1)),
            pl.BlockSpec((None, 1, LANES), lambda i: (layer, 0, 0)),
            pl.BlockSpec(scatter.shape, lambda i: (0, 0, 0)),
            pl.BlockSpec(const.shape, lambda i: (0, 0)),
        ],
        out_specs=[
            pl.BlockSpec((N_SLABS, tm, LANES), lambda i: (0, i, 0)),
            pl.BlockSpec((tm, LANES), lambda i: (i, 0)),
            pl.BlockSpec((tm, LANES), lambda i: (i, 0)),
        ],
        out_shape=[jax.ShapeDtypeStruct((N_SLABS, t, LANES), BF16), bias_out, bias_out],
        scratch_shapes=[pltpu.VMEM((tm, D_MODEL), BF16),
                        pltpu.VMEM((HALO, POOL_WIDTH), F32),
                        pltpu.VMEM((8, LANES), F32)],
        compiler_params=pltpu.CompilerParams(
            dimension_semantics=("arbitrary",), vmem_limit_bytes=VMEM_LIMIT),
        name="in_proj",
    )(h2d, g, w_a, w_b, w_f, bf, scatter, const)


_NT = (((1,), (1,)), ((), ()))


def _fold(x, op):
    out = x[:, :LANES]
    for g in range(1, x.shape[1] // LANES):
        out = op(out, x[:, g * LANES:(g + 1) * LANES])
    return out


def _attn_kernel(q_ref, k_ref, v_ref, z_ref, aq_ref, ak_ref, o_ref,
                 qa_ref, ka_ref, va_ref, s_ref, m_ref, p_ref):
    n_slabs, seq, _ = q_ref.shape
    n_heads = 2 * n_slabs
    tq = ATT_TQ
    lane = lax.broadcasted_iota(jnp.int32, (1, LANES), 1)

    for hh in range(n_heads):
        sl, h = divmod(hh, 2)
        pair = pl.program_id(1) * n_slabs + sl
        first = pair * BIAS_STRIDE + (HEAD_DIM if h == 0 else 0)
        data = (lane < HEAD_DIM) if h == 0 else (lane >= HEAD_DIM)
        dmask = jnp.where(data, 1.0, 0.0).astype(BF16)
        bmask = jnp.where((lane >= first) & (lane < first + BIAS_COLS), 1.0, 0.0).astype(BF16)
        qa_ref[hh] = q_ref[sl] * dmask + aq_ref[0] * bmask
        ka_ref[hh] = k_ref[sl] * dmask + ak_ref[0] * bmask
        va_ref[hh] = v_ref[sl] * dmask + (1.0 - dmask)

    tri = (lax.broadcasted_iota(jnp.int32, (tq, tq), 1)
           <= lax.broadcasted_iota(jnp.int32, (tq, tq), 0))
    lane_t = lax.broadcasted_iota(jnp.int32, (tq, LANES), 1)

    units = [(qi, hh) for qi in reversed(range(seq // tq)) for hh in range(n_heads)]
    unit_end = np.cumsum([qi + 1 for qi, _ in units])
    n_items = int(unit_end[-1])
    ring = s_ref.shape[0]

    def tile(idx):
        return slice(idx * tq, (idx + 1) * tq)

    def score_stream():
        item = 0
        for u, (qi, h) in enumerate(units):
            qt = qa_ref[h, tile(qi), :]
            m_run = None
            for kc in range(qi + 1):
                s = lax.dot_general(qt, ka_ref[h, tile(kc), :], _NT, preferred_element_type=F32)
                if kc == qi:
                    s = jnp.where(tri, s, -jnp.inf)
                s_ref[item % ring] = s
                fm = _fold(s, jnp.maximum)
                m_run = fm if m_run is None else jnp.maximum(m_run, fm)
                item += 1
                if kc == qi:
                    m_ref[u] = jnp.broadcast_to(jnp.max(m_run, axis=1, keepdims=True),
                                                (tq, LANES))
                yield

    def value_stream():
        item = 0
        outs = {}
        for u, (qi, h) in enumerate(units):
            m = m_ref[u]
            for kc in range(qi + 1):
                s = s_ref[item % ring]
                p_ref[u % 2, :, tile(kc)] = jnp.concatenate(
                    [jnp.exp2(s[:, g * LANES:(g + 1) * LANES] - m) for g in range(tq // LANES)],
                    axis=1).astype(BF16)
                item += 1
                if kc == qi:
                    prefix = (qi + 1) * tq
                    acc = jnp.dot(p_ref[u % 2, :, :prefix], va_ref[h, :prefix, :],
                                  preferred_element_type=F32)
                    outs[h] = acc / pltpu.roll(acc, HEAD_DIM, 1)
                    if h % 2 == 1:
                        o = jnp.where(lane_t < HEAD_DIM, outs[h - 1], outs[h])
                        z = z_ref[h // 2, tile(qi), :].astype(F32)
                        o_ref[h // 2, tile(qi), :] = (o * _silu(z)).astype(BF16)
                yield

    scores, values = score_stream(), value_stream()
    scored = 0
    for item in range(n_items):
        unit = int(np.searchsorted(unit_end, item, side="right"))
        target = min(n_items, int(unit_end[unit]) + SCORE_LEAD)
        assert target - item <= ring
        while scored < target:
            next(scores)
            scored += 1
        next(values)


def _attention(pj, aq, ak):
    _, b, s, _ = pj.shape
    n = ATT_SLABS
    heads = 2 * n

    def slab_spec(col):
        return pl.BlockSpec((n, None, s, LANES),
                            lambda i, j: (col * (SLABS_PER_COL // n) + j, i, 0, 0))

    bias_spec = pl.BlockSpec((1, s, LANES), lambda i, j: (i, 0, 0))
    return pl.pallas_call(
        _attn_kernel,
        grid=(b, PAIRS // n),
        in_specs=[slab_spec(COL_Q), slab_spec(COL_K), slab_spec(COL_V), slab_spec(COL_ZA),
                  bias_spec, bias_spec],
        out_specs=pl.BlockSpec((n, None, s, LANES), lambda i, j: (j, i, 0, 0)),
        out_shape=jax.ShapeDtypeStruct((PAIRS, b, s, LANES), BF16),
        scratch_shapes=[
            pltpu.VMEM((heads, s, LANES), BF16),
            pltpu.VMEM((heads, s, LANES), BF16),
            pltpu.VMEM((heads, s, LANES), BF16),
            pltpu.VMEM((SCORE_RING, ATT_TQ, ATT_TQ), F32),
            pltpu.VMEM((heads * (s // ATT_TQ), ATT_TQ, LANES), F32),
            pltpu.VMEM((2, ATT_TQ, s), BF16),
        ],
        compiler_params=pltpu.CompilerParams(
            dimension_semantics=("arbitrary", "arbitrary"), vmem_limit_bytes=VMEM_LIMIT),
        name="fox_attn",
    )(pj, pj, pj, pj, aq, ak)


def _mix_kernel(g_ref, pooled_ref, zp_ref, h_ref, p_ref, wpool_ref, ps_ref, wout_ref,
                gpost_ref, wpg_ref, wpe_ref, out_ref):
    slabs_per_group = POOL_GROUP_DIM // LANES
    mixed = []
    for g in range(len(POOL_WINDOWS)):
        lhs = jnp.concatenate([pooled_ref[sl] for sl in range(g * slabs_per_group,
                                                              (g + 1) * slabs_per_group)], axis=1)
        mixed.append(jnp.dot(lhs, wpool_ref[g], preferred_element_type=F32))
    zp = jnp.concatenate([zp_ref[sl] for sl in range(SLABS_PER_COL)], axis=1).astype(F32)
    pool_out = (jnp.concatenate(mixed, axis=1) * ps_ref[...] * _silu(zp)).astype(BF16)

    attn_out = jnp.concatenate([g_ref[sl] for sl in range(PAIRS)], axis=1)
    mix = jnp.dot(attn_out, wout_ref[:ATTN_WIDTH, :], preferred_element_type=F32)
    mix = mix + jnp.dot(pool_out, wout_ref[ATTN_WIDTH:, :], preferred_element_type=F32)
    h1 = h_ref[...] + _rms_norm(mix, gpost_ref[...])
    gate = jax.nn.sigmoid(jnp.dot(h1.astype(BF16), wpg_ref[...], preferred_element_type=F32))
    pe = jnp.dot(p_ref[...].astype(BF16), wpe_ref[...], preferred_element_type=F32)
    out_ref[...] = h1 + gate * pe


def _mix_out(g, pj, h, p, w_pool, pool_scale, w_out, g_post, w_pg, w_pe, layer):
    b, s, _ = h.shape
    tm = MIX_TM

    def resident(arr):
        block = (None,) + arr.shape[1:]
        return pl.BlockSpec(block, lambda i, j: (layer,) + (0,) * (arr.ndim - 1),
                            pipeline_mode=pl.Buffered(1))

    def col_spec(col):
        return pl.BlockSpec((SLABS_PER_COL, None, tm, LANES), lambda i, j: (col, i, j, 0))

    return pl.pallas_call(
        _mix_kernel,
        grid=(b, s // tm),
        in_specs=[
            col_spec(0), col_spec(COL_U), col_spec(COL_ZP),
            pl.BlockSpec((None, tm, D_MODEL), lambda i, j: (i, j, 0)),
            pl.BlockSpec((None, None, tm, PLE_DIM), lambda i, j: (layer, i, j, 0)),
            resident(w_pool), resident(pool_scale), resident(w_out), resident(g_post),
            resident(w_pg), resident(w_pe),
        ],
        out_specs=pl.BlockSpec((None, tm, D_MODEL), lambda i, j: (i, j, 0)),
        out_shape=jax.ShapeDtypeStruct((b, s, D_MODEL), F32),
        compiler_params=pltpu.CompilerParams(
            dimension_semantics=("arbitrary", "arbitrary"), vmem_limit_bytes=VMEM_LIMIT),
        name="mix_out",
    )(g, pj, pj, h, p, w_pool, pool_scale, w_out, g_post, w_pg, w_pe)


def kernel(x, p, norm_pre, norm_post, w_in, b_f, w_pool, pool_scale, w_out, w_pg, w_pe):
    b, s, d = x.shape
    depth = w_in.shape[0]
    f_lo = 4 * ATTN_WIDTH
    f_hi = f_lo + HEADS
    lane_pad = LANES - HEADS
    w_in = w_in.astype(BF16)
    w_a = w_in
    w_b = w_in[:, :, f_hi:]
    w_f = jnp.pad(w_in[:, :, f_lo:f_hi], ((0, 0), (0, 0), (0, lane_pad)))
    bf = jnp.pad(b_f, ((0, 0), (0, lane_pad))).reshape(depth, 1, LANES)
    g_pre = norm_pre.reshape(depth, 1, d)
    g_post = norm_post.reshape(depth, 1, d)
    ps = pool_scale.reshape(depth, 1, POOL_WIDTH)
    w_pool, w_out, w_pg, w_pe = (w.astype(BF16) for w in (w_pool, w_out, w_pg, w_pe))

    h = x
    for layer in range(depth):
        pj, aq, ak = _in_proj(h.reshape(b * s, d), g_pre, w_a, w_b, w_f, bf, layer, s)
        pj = pj.reshape(N_SLABS, b, s, LANES)
        g = _attention(pj, aq.reshape(b, s, LANES), ak.reshape(b, s, LANES))
        h = _mix_out(g, pj, h, p, w_pool, ps, w_out, g_post, w_pg, w_pe, layer)
    return h
```
